```python
import jax, jax.numpy as jnp
from jax import lax
import numpy as np

D_MODEL = 1024
BATCH = 32
SEQ = 256
DEPTH = 4
DEC_BATCH = 4
DEC_SEQ = 2048
PAST_LEN = 512

GRID_W = 64
HEAD_DIM = 64
N_MOD = 6
EPS = 1e-6
ROPE_BASE = 10000.0
NEG = -1e30
MLA_HEADS = 12
Q_LORA = 384
KV_LORA = 256
QK_NOPE = 64
QK_ROPE = 32
V_DIM = 64
MLA_SCALE = (QK_NOPE + QK_ROPE) ** -0.5
MLA_QBLOCK = 128
POOL_WINDOWS = (2, 4, 8, 16)
POOL_GROUP = D_MODEL // 16
POOL_DIM = 4 * POOL_GROUP
NA_HEADS = 8
NA_KH_MAX = 8
NA_KW = 16
NA_QCB = 16
NA_KCB = 32
SWA_HEADS = 8
SWA_KV_HEADS = 2
SWA_GROUP = SWA_HEADS // SWA_KV_HEADS
SWA_WINDOW = 128
SWA_BLOCK = 128
D_FF = 2816
CONV_W = 3
EVEN_IN = Q_LORA + KV_LORA + QK_ROPE + POOL_DIM
ODD_IN = 3 * NA_HEADS * HEAD_DIM + (SWA_HEADS + 2 * SWA_KV_HEADS) * HEAD_DIM
MIX_WIDTH = MLA_HEADS * V_DIM + POOL_DIM
N_EVEN = (DEPTH + 1) // 2
N_ODD = DEPTH // 2

kernel_name = 'hybrid_diffusion_trunk_step'


def rmsnorm(x, g):
    xf = x.astype(jnp.float32)
    y = xf * lax.rsqrt(jnp.mean(xf * xf, axis=-1, keepdims=True) + EPS)
    return (y * g.astype(jnp.float32)).astype(x.dtype)


def adaln(cvec, w_mod, b_mod):
    m = jax.nn.silu(cvec) @ w_mod + b_mod
    return m.reshape(cvec.shape[0], N_MOD, D_MODEL)[:, :, None, :]


def modulate(h, shift, scale):
    return h * (1 + scale) + shift


def rope_2d(x):
    n, r = x.shape[1], x.shape[-1]
    half = r // 2
    nf = half // 2
    t = jnp.arange(n)
    freqs = ROPE_BASE ** (-jnp.arange(nf, dtype=jnp.float32) / nf)

    def rot(xp, pos):
        ang = pos.astype(jnp.float32)[:, None] * freqs
        cos, sin = jnp.cos(ang)[:, None, :], jnp.sin(ang)[:, None, :]
        x1, x2 = xp[..., :nf], xp[..., nf:]
        return jnp.concatenate([x1 * cos - x2 * sin, x1 * sin + x2 * cos], axis=-1)

    out = jnp.concatenate([rot(x[..., :half], t // GRID_W), rot(x[..., half:], t % GRID_W)], axis=-1)
    return out.astype(x.dtype)


def attend(q, k, v, sink=None):
    d = q.shape[-1]
    s = jnp.einsum('bqhgd,bkhd->bhgqk', q, k).astype(jnp.float32) * (d ** -0.5)
    if sink is not None:
        sk = sink.reshape(1, SWA_KV_HEADS, SWA_GROUP, 1, 1).astype(jnp.float32)
        s = jnp.concatenate([s, jnp.broadcast_to(sk, s.shape[:-1] + (1,))], axis=-1)
    p = jax.nn.softmax(s, axis=-1)
    if sink is not None:
        p = p[..., :-1]
    return jnp.einsum('bhgqk,bkhd->bqhgd', p.astype(v.dtype), v)


def multiscale_pool(xp, w_pool, pool_scale):
    b, n, _ = xp.shape
    xf = xp.astype(jnp.float32)
    cs = jnp.concatenate([jnp.zeros_like(xf[:, :1]), jnp.cumsum(xf, axis=1)], axis=1)
    t = np.arange(n)
    outs = []
    for g, w in enumerate(POOL_WINDOWS):
        lo = np.clip(t - w // 2, 0, n)
        hi = np.clip(t + w // 2, 0, n)
        sl = slice(g * POOL_GROUP, (g + 1) * POOL_GROUP)
        seg = cs[:, :, sl]
        cnt = jnp.asarray((hi - lo).astype(np.float32))[None, :, None]
        outs.append((seg[:, hi] - seg[:, lo]) / cnt - xf[:, :, sl])
    p = jnp.stack(outs, axis=2)
    y = jnp.einsum('bngc,gcd->bngd', p, w_pool.astype(jnp.float32)).reshape(b, n, POOL_DIM)
    return (y * pool_scale.astype(jnp.float32)).astype(xp.dtype)


def mla_split(z):
    a = Q_LORA
    b_ = a + KV_LORA
    c_ = b_ + QK_ROPE
    return z[..., :a], z[..., a:b_], z[..., b_:c_], z[..., c_:]


def mla_queries(qa, g_q, w_uq, positional):
    b, n, _ = qa.shape
    q = (rmsnorm(qa, g_q) @ w_uq).reshape(b, n, MLA_HEADS, QK_NOPE + QK_ROPE)
    q_nope, q_rope = q[..., :QK_NOPE], q[..., QK_NOPE:]
    if positional:
        q_rope = rope_2d(q_rope)
    return q_nope, q_rope


def mla_expand(latent, w_ukv):
    b, n, _ = latent.shape
    kv = (latent[..., :KV_LORA] @ w_ukv).reshape(b, n, MLA_HEADS, QK_NOPE + V_DIM)
    return kv[..., :QK_NOPE], latent[..., KV_LORA:], kv[..., QK_NOPE:]


def mla_attend(q_nope, q_rope, k_nope, k_rope, v):
    s = (jnp.einsum('bqhd,bkhd->bhqk', q_nope, k_nope)
         + jnp.einsum('bqhr,bkr->bhqk', q_rope, k_rope)).astype(jnp.float32) * MLA_SCALE
    p = jax.nn.softmax(s, axis=-1).astype(v.dtype)
    return jnp.einsum('bhqk,bkhd->bqhd', p, v)


def even_context(h, w_in, g_q, g_kv, w_uq, w_ukv, w_pool, pool_scale, w_out):
    b, n, _ = h.shape
    qa, ckv, kr, xp = mla_split(h @ w_in)
    latent = jnp.concatenate([rmsnorm(ckv, g_kv), kr], axis=-1)
    q_nope, q_rope = mla_queries(qa, g_q, w_uq, False)
    k_nope, k_rope, v = mla_expand(latent, w_ukv)
    o = mla_attend(q_nope, q_rope, k_nope, k_rope, v).reshape(b, n, MLA_HEADS * V_DIM)
    y_pool = multiscale_pool(xp, w_pool, pool_scale)
    return jnp.concatenate([o, y_pool], axis=-1) @ w_out, latent


def even_latent(h, lat_ctx, w_in, g_q, g_kv, w_uq, w_ukv, w_pool, pool_scale, w_out):
    b, n, _ = h.shape
    qa, ckv, kr, xp = mla_split(h @ w_in)
    kr = rope_2d(kr[:, :, None, :])[:, :, 0, :]
    lat = jnp.concatenate([rmsnorm(ckv, g_kv), kr], axis=-1)
    q_nope, q_rope = mla_queries(qa, g_q, w_uq, True)
    k_nope, k_rope, v = mla_expand(jnp.concatenate([lat, lat_ctx.astype(lat.dtype)], axis=1), w_ukv)
    nb = n // MLA_QBLOCK
    qn_b = q_nope.reshape(b, nb, MLA_QBLOCK, MLA_HEADS, QK_NOPE).swapaxes(0, 1)
    qr_b = q_rope.reshape(b, nb, MLA_QBLOCK, MLA_HEADS, QK_ROPE).swapaxes(0, 1)
    o = lax.map(lambda qs: mla_attend(qs[0], qs[1], k_nope, k_rope, v), (qn_b, qr_b))
    o = o.swapaxes(0, 1).reshape(b, n, MLA_HEADS * V_DIM)
    y_pool = multiscale_pool(xp, w_pool, pool_scale)
    return jnp.concatenate([o, y_pool], axis=-1) @ w_out


def odd_split(z):
    b, n, _ = z.shape
    dna = NA_HEADS * HEAD_DIM
    dsq = SWA_HEADS * HEAD_DIM
    dkv = SWA_KV_HEADS * HEAD_DIM
    offs = [int(o) for o in np.cumsum([0, dna, dna, dna, dsq, dkv, dkv])]
    parts = [z[..., offs[i]:offs[i + 1]] for i in range(6)]
    q_na = parts[0].reshape(b, n, NA_HEADS, HEAD_DIM)
    k_na = parts[1].reshape(b, n, NA_HEADS, HEAD_DIM)
    v_na = parts[2].reshape(b, n, NA_HEADS, HEAD_DIM)
    q_sw = parts[3].reshape(b, n, SWA_KV_HEADS, SWA_GROUP, HEAD_DIM)
    k_sw = parts[4].reshape(b, n, SWA_KV_HEADS, HEAD_DIM)
    v_sw = parts[5].reshape(b, n, SWA_KV_HEADS, HEAD_DIM)
    return q_na, k_na, v_na, q_sw, k_sw, v_sw


def na_latent(q, k, v, k_ctx, v_ctx, rpb):
    b, n, hh, d = q.shape
    rows = n // GRID_W
    kh = min(NA_KH_MAX, rows)
    ncb = GRID_W // NA_QCB
    row_start = np.clip(np.arange(rows) - kh // 2, 0, rows - kh)
    dr_idx = row_start[:, None] + np.arange(kh)[None, :] - np.arange(rows)[:, None] + NA_KH_MAX - 1
    qcol = np.arange(GRID_W).reshape(ncb, NA_QCB)
    kcol_start = np.clip(np.arange(ncb) * NA_QCB - (NA_KCB - NA_QCB) // 2, 0, GRID_W - NA_KCB)
    kcol = kcol_start[:, None] + np.arange(NA_KCB)[None, :]
    qcol_start = np.clip(qcol - NA_KW // 2, 0, GRID_W - NA_KW)
    col_mask = (kcol[:, None, :] >= qcol_start[..., None]) & (kcol[:, None, :] < qcol_start[..., None] + NA_KW)
    dc_idx = np.clip(kcol[:, None, :] - qcol[..., None] + NA_KW - 1, 0, 2 * NA_KW - 2)
    kg = k.reshape(b, rows, GRID_W, hh, d)
    vg = v.reshape(b, rows, GRID_W, hh, d)
    qg = q.reshape(b, rows, ncb, NA_QCB, hh, d).swapaxes(0, 1)
    scale = d ** -0.5
    nloc = kh * NA_KCB
    mask = jnp.asarray(col_mask)[:, :, None, :]

    def row_block(args):
        q_r, r0, dr = args
        k_r = lax.dynamic_slice_in_dim(kg, r0, kh, axis=1)[:, :, kcol]
        v_r = lax.dynamic_slice_in_dim(vg, r0, kh, axis=1)[:, :, kcol]
        s_loc = jnp.einsum('bcqhd,bkcjhd->bhcqkj', q_r, k_r).astype(jnp.float32) * scale
        bias = rpb[:, dr[None, None, :, None], dc_idx[:, :, None, :]].astype(jnp.float32)
        s_loc = jnp.where(mask, s_loc + bias, NEG)
        s_ctx = jnp.einsum('bcqhd,blhd->bhcql', q_r, k_ctx).astype(jnp.float32) * scale
        s = jnp.concatenate([s_loc.reshape(b, hh, ncb, NA_QCB, nloc), s_ctx], axis=-1)
        p = jax.nn.softmax(s, axis=-1).astype(v.dtype)
        p_loc = p[..., :nloc].reshape(b, hh, ncb, NA_QCB, kh, NA_KCB)
        return (jnp.einsum('bhcqkj,bkcjhd->bcqhd', p_loc, v_r)
                + jnp.einsum('bhcql,blhd->bcqhd', p[..., nloc:], v_ctx))

    o = lax.map(row_block, (qg, jnp.asarray(row_start, jnp.int32), jnp.asarray(dr_idx, jnp.int32)))
    return o.swapaxes(0, 1).reshape(b, n, hh * d)


def swa_latent(q, k, v, k_ctx, v_ctx, sink):
    b, n, hk, g, d = q.shape
    nb = n // SWA_BLOCK
    span = SWA_BLOCK + 2 * SWA_WINDOW
    scale = d ** -0.5
    qb = q.reshape(b, nb, SWA_BLOCK, hk, g, d)
    padw = ((0, 0), (SWA_WINDOW, SWA_WINDOW), (0, 0), (0, 0))
    idx = np.arange(nb)[:, None] * SWA_BLOCK + np.arange(span)[None, :]
    kb = jnp.pad(k, padw)[:, idx]
    vb = jnp.pad(v, padw)[:, idx]
    tq = np.arange(nb)[:, None, None] * SWA_BLOCK + np.arange(SWA_BLOCK)[None, :, None]
    sk = np.arange(nb)[:, None, None] * SWA_BLOCK - SWA_WINDOW + np.arange(span)[None, None, :]
    band = (sk >= 0) & (sk < n) & (np.abs(tq - sk) <= SWA_WINDOW)
    s_loc = jnp.einsum('bnqhgd,bnkhd->bnhgqk', qb, kb).astype(jnp.float32) * scale
    s_loc = jnp.where(jnp.asarray(band)[None, :, None, None], s_loc, NEG)
    s_ctx = jnp.einsum('bnqhgd,blhd->bnhgql', qb, k_ctx).astype(jnp.float32) * scale
    s_sink = jnp.broadcast_to(sink.reshape(1, 1, hk, g, 1, 1).astype(jnp.float32), s_loc.shape[:-1] + (1,))
    s = jnp.concatenate([s_loc, s_ctx, s_sink], axis=-1)
    p = jax.nn.softmax(s, axis=-1).astype(v.dtype)
    o = (jnp.einsum('bnhgqk,bnkhd->bnqhgd', p[..., :span], vb)
         + jnp.einsum('bnhgql,blhd->bnqhgd', p[..., span:-1], v_ctx))
    return o.reshape(b, n, hk * g * d)


def odd_context(h, w_in, sink, w_out):
    b, n, _ = h.shape
    q_na, k_na, v_na, q_sw, k_sw, v_sw = odd_split(h @ w_in)
    o_na = attend(q_na[:, :, :, None], k_na, v_na).reshape(b, n, NA_HEADS * HEAD_DIM)
    o_sw = attend(q_sw, k_sw, v_sw, sink).reshape(b, n, SWA_HEADS * HEAD_DIM)
    out = jnp.concatenate([o_na, o_sw], axis=-1) @ w_out
    return out, jnp.stack([k_na, v_na], axis=2), jnp.stack([k_sw, v_sw], axis=2)


def odd_latent(h, na_kv_ctx, sw_kv_ctx, w_in, rpb, sink, w_out):
    b, n, _ = h.shape
    q_na, k_na, v_na, q_sw, k_sw, v_sw = odd_split(h @ w_in)
    na_kv_ctx = na_kv_ctx.astype(h.dtype)
    sw_kv_ctx = sw_kv_ctx.astype(h.dtype)
    o_na = na_latent(q_na, k_na, v_na, na_kv_ctx[:, :, 0], na_kv_ctx[:, :, 1], rpb)
    q_sw = rope_2d(q_sw.reshape(b, n, SWA_HEADS, HEAD_DIM)).reshape(b, n, SWA_KV_HEADS, SWA_GROUP, HEAD_DIM)
    k_sw = rope_2d(k_sw)
    o_sw = swa_latent(q_sw, k_sw, v_sw, sw_kv_ctx[:, :, 0], sw_kv_ctx[:, :, 1], sink)
    return jnp.concatenate([o_na, o_sw], axis=-1) @ w_out


def conv_ffn(h, w_up, conv_w, conv_b, w_down):
    n = h.shape[1]
    u = h @ w_up
    pad = CONV_W // 2
    up = jnp.pad(u, ((0, 0), (pad, pad), (0, 0)))
    u = sum(up[:, j:j + n] * conv_w[j] for j in range(CONV_W)) + conv_b
    a, gt = u[..., :D_FF], u[..., D_FF:]
    return (jax.nn.silu(gt) * a) @ w_down


def setup_inputs(seed: int = 0) -> dict:
    key = jax.random.key(seed)
    ks = iter(jax.random.split(key, 40))

    def nrm(shape, scale=1.0):
        return jax.random.normal(next(ks), shape, jnp.float32) * scale

    def gain(shape):
        return 1.0 + nrm(shape, 0.05)

    return {
        'x_prompt': nrm((BATCH, SEQ, D_MODEL)),
        'x_sample': nrm((DEC_BATCH, DEC_SEQ, D_MODEL)),
        'cache_mla_latent': nrm((DEC_BATCH, N_EVEN, PAST_LEN, KV_LORA + QK_ROPE)),
        'cache_na_kv': nrm((DEC_BATCH, N_ODD, PAST_LEN, 2, NA_HEADS, HEAD_DIM)),
        'cache_swa_kv': nrm((DEC_BATCH, N_ODD, PAST_LEN, 2, SWA_KV_HEADS, HEAD_DIM)),
        'c': nrm((DEC_BATCH, D_MODEL)),
        'c_ctx': nrm((D_MODEL,)),
        'w_mod': nrm((DEPTH, D_MODEL, N_MOD * D_MODEL), 0.5 * D_MODEL ** -0.5),
        'b_mod': nrm((DEPTH, N_MOD * D_MODEL), 0.02),
        'norm_mix': gain((DEPTH, D_MODEL)),
        'norm_ffn': gain((DEPTH, D_MODEL)),
        'norm_final': gain((D_MODEL,)),
        'w_in_even': nrm((N_EVEN, D_MODEL, EVEN_IN), D_MODEL ** -0.5),
        'mla_q_norm': gain((N_EVEN, Q_LORA)),
        'mla_kv_norm': gain((N_EVEN, KV_LORA)),
        'w_uq': nrm((N_EVEN, Q_LORA, MLA_HEADS * (QK_NOPE + QK_ROPE)), Q_LORA ** -0.5),
        'w_ukv': nrm((N_EVEN, KV_LORA, MLA_HEADS * (QK_NOPE + V_DIM)), KV_LORA ** -0.5),
        'w_pool': nrm((N_EVEN, 4, POOL_GROUP, POOL_GROUP), POOL_GROUP ** -0.5),
        'pool_scale': 1.0 + nrm((N_EVEN, POOL_DIM), 0.1),
        'w_out_even': nrm((N_EVEN, MIX_WIDTH, D_MODEL), MIX_WIDTH ** -0.5),
        'w_in_odd': nrm((N_ODD, D_MODEL, ODD_IN), D_MODEL ** -0.5),
        'na_rpb': nrm((N_ODD, NA_HEADS, 2 * NA_KH_MAX - 1, 2 * NA_KW - 1), 0.1),
        'swa_sink': nrm((N_ODD, SWA_HEADS), 0.5),
        'w_out_odd': nrm((N_ODD, MIX_WIDTH, D_MODEL), MIX_WIDTH ** -0.5),
        'w_up': nrm((DEPTH, D_MODEL, 2 * D_FF), D_MODEL ** -0.5),
        'conv_w': nrm((DEPTH, CONV_W, 2 * D_FF), CONV_W ** -0.5),
        'conv_b': nrm((DEPTH, 2 * D_FF), 0.01),
        'w_down': nrm((DEPTH, D_FF, D_MODEL), D_FF ** -0.5),
    }


def reference(x_prompt, x_sample, cache_mla_latent, cache_na_kv, cache_swa_kv, c, c_ctx,
              w_mod, b_mod, norm_mix, norm_ffn, norm_final,
              w_in_even, mla_q_norm, mla_kv_norm, w_uq, w_ukv, w_pool, pool_scale, w_out_even,
              w_in_odd, na_rpb, swa_sink, w_out_odd,
              w_up, conv_w, conv_b, w_down):
    xp, xs = x_prompt, x_sample
    mla_states, na_states, swa_states = [], [], []
    for l in range(DEPTH):
        e = l // 2
        mp = adaln(c_ctx[None, :], w_mod[l], b_mod[l])
        ms = adaln(c, w_mod[l], b_mod[l])
        hp = modulate(rmsnorm(xp, norm_mix[l]), mp[:, 0], mp[:, 1])
        hs = modulate(rmsnorm(xs, norm_mix[l]), ms[:, 0], ms[:, 1])
        if l % 2 == 0:
            op, lat = even_context(hp, w_in_even[e], mla_q_norm[e], mla_kv_norm[e], w_uq[e], w_ukv[e],
                                   w_pool[e], pool_scale[e], w_out_even[e])
            os_ = even_latent(hs, cache_mla_latent[:, e], w_in_even[e], mla_q_norm[e], mla_kv_norm[e],
                              w_uq[e], w_ukv[e], w_pool[e], pool_scale[e], w_out_even[e])
            mla_states.append(lat)
        else:
            op, kv_na, kv_sw = odd_context(hp, w_in_odd[e], swa_sink[e], w_out_odd[e])
            os_ = odd_latent(hs, cache_na_kv[:, e], cache_swa_kv[:, e], w_in_odd[e], na_rpb[e],
                             swa_sink[e], w_out_odd[e])
            na_states.append(kv_na)
            swa_states.append(kv_sw)
        xp = xp + mp[:, 2] * op
        xs = xs + ms[:, 2] * os_
        hp = modulate(rmsnorm(xp, norm_ffn[l]), mp[:, 3], mp[:, 4])
        hs = modulate(rmsnorm(xs, norm_ffn[l]), ms[:, 3], ms[:, 4])
        xp = xp + mp[:, 5] * conv_ffn(hp, w_up[l], conv_w[l], conv_b[l], w_down[l])
        xs = xs + ms[:, 5] * conv_ffn(hs, w_up[l], conv_w[l], conv_b[l], w_down[l])
    y_prompt = rmsnorm(xp, norm_final)
    y_sample = rmsnorm(xs, norm_final)
    new_mla_latent = jnp.stack(mla_states, axis=1)
    new_na_kv = jnp.stack(na_states, axis=1)
    new_swa_kv = jnp.stack(swa_states, axis=1)
    return (y_prompt, y_sample, new_mla_latent, new_na_kv, new_swa_kv)
```

```python
import functools

import numpy as np
import jax
import jax.numpy as jnp
from jax import lax
from jax.experimental import pallas as pl
from jax.experimental.pallas import tpu as pltpu

F32 = jnp.float32
BF16 = jnp.bfloat16

D_MODEL = 1024
GRID_W = 64
N_MOD = 6
EPS = 1e-6
ROPE_BASE = 10000.0
NEG = -1e30
MLA_HEADS = 12
Q_LORA = 384
KV_LORA = 256
QK_NOPE = 64
QK_ROPE = 32
V_DIM = 64
MLA_SCALE = (QK_NOPE + QK_ROPE) ** -0.5
POOL_WINDOWS = (2, 4, 8, 16)
POOL_GROUP = 64
POOL_DIM = 256
NA_HEADS = 8
NA_KH = 8
NA_KW = 16
SWA_HEADS = 8
SWA_KV_HEADS = 2
SWA_WINDOW = 128
HEAD_DIM = 64
D_FF = 2816
PAST_LEN = 512

LANES = 128
TM = 256
FF_CHUNK = 256
N_FF_CHUNKS = D_FF // FF_CHUNK
POOL_PAD = 16
VMEM_LIMIT = 60000 * 1024


def _cparams(sem):
    return pltpu.CompilerParams(dimension_semantics=sem, vmem_limit_bytes=VMEM_LIMIT)


def _rms(x, g):
    return x * lax.rsqrt(jnp.mean(x * x, axis=-1, keepdims=True) + EPS) * g


def _silu(x):
    return x * (1.0 / (1.0 + jnp.exp(-x)))


def _dot(a, b):
    return jnp.dot(a, b, preferred_element_type=F32)


def _dot_t(a, b):
    return lax.dot_general(a, b, (((1,), (1,)), ((), ())), preferred_element_type=F32)


def _rope(x, cos, sin_prev, sin_next, shift):
    w = x.shape[-1]
    return x * cos + pltpu.roll(x, shift, 1) * sin_prev + pltpu.roll(x, w - shift, 1) * sin_next


def _const_spec(shape):
    nd = len(shape)
    return pl.BlockSpec(shape, lambda *_: (0,) * nd, pipeline_mode=pl.Buffered(1))


def _adaln_body(c_ref, w_ref, b_ref, o_ref):
    a = _silu(c_ref[...]).astype(BF16)
    o_ref[0] = _dot(a, w_ref[0].astype(BF16)) + b_ref[0]


def _adaln(c_all, w_mod, b_mod):
    depth, _, width = w_mod.shape
    tn = 1536
    return pl.pallas_call(
        _adaln_body,
        out_shape=jax.ShapeDtypeStruct((depth, 8, width), F32),
        grid=(depth, width // tn),
        in_specs=[
            pl.BlockSpec((8, D_MODEL), lambda l, j: (0, 0)),
            pl.BlockSpec((1, D_MODEL, tn), lambda l, j: (l, 0, j)),
            pl.BlockSpec((1, 1, tn), lambda l, j: (l, 0, j)),
        ],
        out_specs=pl.BlockSpec((1, 8, tn), lambda l, j: (l, 0, j)),
        compiler_params=_cparams(("arbitrary", "arbitrary")),
        name="adaln",
    )(c_all, w_mod, b_mod.reshape(depth, 1, width))


def _ffn_body(x_ref, xp_ref, xn_ref, mod_ref, g_ref, wup_ref, cw_ref, cb_ref, wdn_ref, gf_ref,
              o_ref, acc_ref, *, n, final):
    i = pl.program_id(0)
    m = mod_ref[0]
    shift, scale, gate = m[3:4], m[4:5], m[5:6]
    g = g_ref[...]

    def hn(x):
        return _rms(x, g) * (1.0 + scale) + shift

    x = x_ref[...]
    hext = jnp.concatenate([hn(xp_ref[...]), hn(x), hn(xn_ref[...])], axis=0).astype(BF16)
    pos = (i * TM + lax.broadcasted_iota(jnp.int32, (TM, 1), 0)) & (n - 1)
    not_first = (pos != 0).astype(F32)
    not_last = (pos != n - 1).astype(F32)
    acc_ref[...] = jnp.zeros_like(acc_ref)

    def chunk(j, carry):
        u = _dot(hext, wup_ref[j])
        up = pltpu.roll(u, 1, 0)[8:8 + TM]
        un = pltpu.roll(u, TM + 15, 0)[8:8 + TM]
        cw = cw_ref[j]
        v = up * not_first * cw[0:1] + u[8:8 + TM] * cw[1:2] + un * not_last * cw[2:3] + cb_ref[j]
        act = (_silu(v[:, FF_CHUNK:]) * v[:, :FF_CHUNK]).astype(BF16)
        acc_ref[...] += _dot(act, wdn_ref[j])
        return carry

    lax.fori_loop(0, N_FF_CHUNKS, chunk, 0)
    y = x + gate * acc_ref[...]
    if final:
        y = _rms(y, gf_ref[...])
    o_ref[...] = y


def _ffn(x, mod_l, n, gbase, per_seq, g_ffn, wup, cw, cb, wdn, g_final, final):
    t = x.shape[0]
    nblk8 = t // 8
    tiles_per_seq = n // TM if per_seq else 0

    def gmap(i):
        return (gbase + (i // tiles_per_seq if per_seq else 0), 0, 0)

    return pl.pallas_call(
        functools.partial(_ffn_body, n=n, final=final),
        out_shape=jax.ShapeDtypeStruct((t, D_MODEL), F32),
        grid=(t // TM,),
        in_specs=[
            pl.BlockSpec((TM, D_MODEL), lambda i: (i, 0)),
            pl.BlockSpec((8, D_MODEL), lambda i: (jnp.maximum(i * (TM // 8) - 1, 0), 0)),
            pl.BlockSpec((8, D_MODEL), lambda i: (jnp.minimum((i + 1) * (TM // 8), nblk8 - 1), 0)),
            pl.BlockSpec((1, N_MOD, D_MODEL), gmap),
            _const_spec((1, D_MODEL)),
            _const_spec(wup.shape),
            _const_spec(cw.shape),
            _const_spec(cb.shape),
            _const_spec(wdn.shape),
            _const_spec((1, D_MODEL)),
        ],
        out_specs=pl.BlockSpec((TM, D_MODEL), lambda i: (i, 0)),
        scratch_shapes=[pltpu.VMEM((TM, D_MODEL), F32)],
        compiler_params=_cparams(("arbitrary",)),
        name="conv_ffn",
    )(x, x, x, mod_l, g_ffn, wup, cw, cb, wdn, g_final)


def _even_body(*refs, n, latent):
    if latent:
        (x_ref, mod_ref, g_ref, win_ref, gq_ref, gkv_ref, wuq_ref, wkc_ref, wuv_ref, wpool_ref,
         pscale_ref, cnt_ref, wout_ref, cos_ref, sp_ref, sn_ref, ctx_ref,
         o_ref, qs, ks, vs, xps, osc) = refs
        lat_ref = None
    else:
        (x_ref, mod_ref, g_ref, win_ref, gq_ref, gkv_ref, wuq_ref, wkc_ref, wuv_ref, wpool_ref,
         pscale_ref, cnt_ref, wout_ref,
         o_ref, lat_ref, qs, ks, vs, xps, osc) = refs
    nt = n // TM
    s = pl.program_id(1)
    m = mod_ref[0]

    @pl.when(s < nt)
    def _project():
        r0 = pl.multiple_of(s * TM, TM)
        h = (_rms(x_ref[...], g_ref[...]) * (1.0 + m[1:2]) + m[0:1]).astype(BF16)
        z = _dot(h, win_ref[...])
        qn = _rms(z[:, :Q_LORA], gq_ref[...]).astype(BF16)
        q = _dot(qn, wuq_ref[...])
        latc = _rms(z[:, Q_LORA:Q_LORA + KV_LORA], gkv_ref[...])
        krp = z[:, 896:1024]
        if latent:
            cos, sp, sn = cos_ref[...], sp_ref[...], sn_ref[...]
            krp = _rope(krp, cos, sp, sn, 8)
        latcat = jnp.concatenate([latc, krp], axis=1)
        if lat_ref is not None:
            lat_ref[...] = latcat
        lb = latcat.astype(BF16)
        kc = _dot(lb, wkc_ref[...])
        v = _dot(lb[:, :KV_LORA], wuv_ref[...])
        for hd in range(MLA_HEADS):
            qh = q[:, LANES * hd:LANES * (hd + 1)]
            if latent:
                qh = _rope(qh, cos, sp, sn, 8)
            qs[hd, pl.ds(r0, TM), :] = qh.astype(BF16)
            ks[hd, pl.ds(r0, TM), :] = kc[:, LANES * hd:LANES * (hd + 1)].astype(BF16)
        for p in range(MLA_HEADS // 2):
            vs[p, pl.ds(r0, TM), :] = v[:, LANES * p:LANES * (p + 1)].astype(BF16)
        xps[pl.ds(POOL_PAD + r0, TM), :] = z[:, 640:896]

        @pl.when(s == 0)
        def _first():
            xps[0:POOL_PAD, :] = jnp.zeros((POOL_PAD, POOL_DIM), F32)
            xps[n + POOL_PAD:n + 2 * POOL_PAD, :] = jnp.zeros((POOL_PAD, POOL_DIM), F32)
            if latent:
                cb = ctx_ref[0]
                kcc = _dot(cb, wkc_ref[...])
                vc = _dot(cb[:, :KV_LORA], wuv_ref[...])
                for hd in range(MLA_HEADS):
                    ks[hd, n:n + PAST_LEN, :] = kcc[:, LANES * hd:LANES * (hd + 1)].astype(BF16)
                for p in range(MLA_HEADS // 2):
                    vs[p, n:n + PAST_LEN, :] = vc[:, LANES * p:LANES * (p + 1)].astype(BF16)

    @pl.when(s >= nt)
    def _attend():
        q0 = pl.multiple_of((s - nt) * TM, TM)
        lane = lax.broadcasted_iota(jnp.int32, (TM, LANES), 1)

        def pair(p, carry):
            vp = vs[p]
            outs = []
            for a in (0, 1):
                hd = 2 * p + a
                sc = _dot_t(qs[hd, pl.ds(q0, TM), :], ks[hd])
                mx = jnp.max(sc, axis=-1, keepdims=True)
                e = jnp.exp((sc - mx) * MLA_SCALE)
                l = jnp.sum(e, axis=-1, keepdims=True)
                outs.append(_dot(e.astype(BF16), vp) / l)
            osc[p] = jnp.where(lane < V_DIM, outs[0], outs[1]).astype(BF16)
            return carry

        lax.fori_loop(0, MLA_HEADS // 2, pair, 0)

        rows = TM + 2 * POOL_PAD
        a0 = xps[pl.ds(q0, rows), :]
        s1 = a0 + pltpu.roll(a0, 1, 0)
        s2 = s1 + pltpu.roll(s1, 2, 0)
        s4 = s2 + pltpu.roll(s2, 4, 0)
        s8 = s4 + pltpu.roll(s4, 8, 0)
        pl_lane = lax.broadcasted_iota(jnp.int32, (rows, POOL_DIM), 1)
        win = jnp.where(pl_lane < 64, s1,
                        jnp.where(pl_lane < 128, pltpu.roll(s2, rows - 1, 0),
                                  jnp.where(pl_lane < 192, pltpu.roll(s4, rows - 3, 0),
                                            pltpu.roll(s8, rows - 7, 0))))
        pooled = win[POOL_PAD:POOL_PAD + TM] / cnt_ref[...] - a0[POOL_PAD:POOL_PAD + TM]
        ypool = _dot(pooled.astype(BF16), wpool_ref[...]) * pscale_ref[...]

        mix = jnp.concatenate([osc[p] for p in range(MLA_HEADS // 2)] + [ypool.astype(BF16)], axis=1)
        o_ref[...] = x_ref[...] + m[2:3] * _dot(mix, wout_ref[...])


def _even_mixer(x, mod_l, n, gbase, latent, g_mix, w, tables, ctx):
    t = x.shape[0]
    nb = t // n
    nt = n // TM
    nk = n + (PAST_LEN if latent else 0)

    def xmap(b, s):
        return (b * nt + s % nt, 0)

    def gmap(b, s):
        return (gbase + (b if latent else 0), 0, 0)

    in_specs = [
        pl.BlockSpec((TM, D_MODEL), xmap),
        pl.BlockSpec((1, N_MOD, D_MODEL), gmap),
        _const_spec((1, D_MODEL)),
        _const_spec(w["w_in"].shape),
        _const_spec((1, Q_LORA)),
        _const_spec((1, KV_LORA)),
        _const_spec(w["w_uq"].shape),
        _const_spec(w["w_kcat"].shape),
        _const_spec(w["w_uv"].shape),
        _const_spec(w["w_pool"].shape),
        _const_spec((1, POOL_DIM)),
        pl.BlockSpec((TM, POOL_DIM), lambda b, s: (jnp.maximum(s - nt, 0), 0)),
        _const_spec(w["w_out"].shape),
    ]
    args = [x, mod_l, g_mix, w["w_in"], w["g_q"], w["g_kv"], w["w_uq"], w["w_kcat"], w["w_uv"],
            w["w_pool"], w["pool_scale"], tables["cnt"], w["w_out"]]
    out_shape = [jax.ShapeDtypeStruct((t, D_MODEL), F32)]
    out_specs = [pl.BlockSpec((TM, D_MODEL), lambda b, s: (b * nt + jnp.maximum(s - nt, 0), 0))]
    if latent:
        tspec = pl.BlockSpec((TM, LANES), lambda b, s: (jnp.minimum(s, nt - 1), 0))
        in_specs += [tspec, tspec, tspec, pl.BlockSpec((1, PAST_LEN, 384), lambda b, s: (b, 0, 0))]
        args += [tables["cos"], tables["sin_prev"], tables["sin_next"], ctx]
    else:
        out_shape.append(jax.ShapeDtypeStruct((t, 384), F32))
        out_specs.append(pl.BlockSpec((TM, 384), lambda b, s: (b * nt + jnp.minimum(s, nt - 1), 0)))
    scratch = [
        pltpu.VMEM((MLA_HEADS, n, LANES), BF16),
        pltpu.VMEM((MLA_HEADS, nk, LANES), BF16),
        pltpu.VMEM((MLA_HEADS // 2, nk, LANES), BF16),
        pltpu.VMEM((n + 2 * POOL_PAD, POOL_DIM), F32),
        pltpu.VMEM((MLA_HEADS // 2, TM, LANES), BF16),
    ]
    return pl.pallas_call(
        functools.partial(_even_body, n=n, latent=latent),
        out_shape=out_shape,
        grid=(nb, 2 * nt),
        in_specs=in_specs,
        out_specs=out_specs,
        scratch_shapes=scratch,
        compiler_params=_cparams(("arbitrary", "arbitrary")),
        name="even_latent" if latent else "even_context",
    )(*args)


def _bias_body(rpb_ref, o_ref):
    h = pl.program_id(0)
    qc = lax.broadcasted_iota(jnp.int32, (GRID_W, GRID_W), 0)
    kc = lax.broadcasted_iota(jnp.int32, (GRID_W, GRID_W), 1)
    c0 = jnp.clip(qc - NA_KW // 2, 0, GRID_W - NA_KW)
    inwin = (kc >= c0) & (kc < c0 + NA_KW)
    dc = kc - qc + NA_KW - 1
    nj = 2 * NA_KW - 1

    def per_dr(dr, carry):
        acc = jnp.full((GRID_W, GRID_W), NEG, F32)
        for j in range(nj):
            acc = jnp.where(inwin & (dc == j), rpb_ref[(h * (2 * NA_KH - 1) + dr) * nj + j], acc)
        o_ref[0, dr] = acc
        return carry

    lax.fori_loop(0, 2 * NA_KH - 1, per_dr, 0)


def _na_bias_blocks(rpb):
    nh, ndr, nj = rpb.shape
    return pl.pallas_call(
        _bias_body,
        out_shape=jax.ShapeDtypeStruct((nh, ndr, GRID_W, GRID_W), F32),
        grid=(nh,),
        in_specs=[pl.BlockSpec(memory_space=pltpu.SMEM)],
        out_specs=pl.BlockSpec((1, ndr, GRID_W, GRID_W), lambda h: (h, 0, 0, 0)),
        compiler_params=_cparams(("arbitrary",)),
        name="na_bias",
    )(rpb.reshape(-1))


def _softmax_pv(parts, extra=None):
    mx = functools.reduce(jnp.maximum, [jnp.max(sc, axis=-1, keepdims=True) for sc, _ in parts])
    if extra is not None:
        mx = jnp.maximum(mx, extra)
    l = jnp.exp(extra - mx) if extra is not None else 0.0
    acc = None
    for sc, val in parts:
        e = jnp.exp(sc - mx)
        l = l + jnp.sum(e, axis=-1, keepdims=True)
        o = val(e.astype(BF16)) if callable(val) else _dot(e.astype(BF16), val)
        acc = o if acc is None else acc + o
    return acc / l


def _odd_body(*refs, n, latent):
    if latent:
        (x_ref, mod_ref, g_ref, win_ref, sink_ref, wout_ref, cos_ref, sp_ref, sn_ref, bias_ref,
         cna_ref, csw_ref,
         o_ref, qna, kna, vna, qsw, ksw, vsw, osc) = refs
        nakv_ref = swkv_ref = None
    else:
        (x_ref, mod_ref, g_ref, win_ref, sink_ref, wout_ref,
         o_ref, nakv_ref, swkv_ref, qna, kna, vna, qsw, ksw, vsw, osc) = refs
    nt = n // TM
    s = pl.program_id(1)
    m = mod_ref[0]
    sw_off = SWA_WINDOW if latent else 0
    lane = lax.broadcasted_iota(jnp.int32, (TM, LANES), 1)
    lo = lane < HEAD_DIM

    @pl.when(s < nt)
    def _project():
        r0 = pl.multiple_of(s * TM, TM)
        h = (_rms(x_ref[...], g_ref[...]) * (1.0 + m[1:2]) + m[0:1]).astype(BF16)
        z = _dot(h, win_ref[...])
        if nakv_ref is not None:
            nakv_ref[...] = z[:, 512:1536]
            swkv_ref[...] = z[:, 2048:2304]
        scale = HEAD_DIM ** -0.5
        for p in range(NA_HEADS // 2):
            qp = z[:, LANES * p:LANES * (p + 1)] * scale
            qna[2 * p, pl.ds(r0, TM), :] = jnp.where(lo, qp, 0.0).astype(BF16)
            qna[2 * p + 1, pl.ds(r0, TM), :] = jnp.where(lo, 0.0, qp).astype(BF16)
            kna[p, pl.ds(r0, TM), :] = z[:, 512 + LANES * p:512 + LANES * (p + 1)].astype(BF16)
            vna[p, pl.ds(r0, TM), :] = z[:, 1024 + LANES * p:1024 + LANES * (p + 1)].astype(BF16)
        if latent:
            cos, sp, sn = cos_ref[...], sp_ref[...], sn_ref[...]
        for p in range(SWA_HEADS // 2):
            qp = z[:, 1536 + LANES * p:1536 + LANES * (p + 1)]
            if latent:
                qp = _rope(qp, cos, sp, sn, 16)
            qp = qp * scale
            qsw[2 * p, pl.ds(r0, TM), :] = jnp.where(lo, qp, 0.0).astype(BF16)
            qsw[2 * p + 1, pl.ds(r0, TM), :] = jnp.where(lo, 0.0, qp).astype(BF16)
        k = z[:, 2048:2176]
        if latent:
            k = _rope(k, cos, sp, sn, 16)
        v = z[:, 2176:2304]
        ksw_, vsw_ = pltpu.roll(k, HEAD_DIM, 1), pltpu.roll(v, HEAD_DIM, 1)
        ksw[0, pl.ds(sw_off + r0, TM), :] = jnp.where(lo, k, ksw_).astype(BF16)
        ksw[1, pl.ds(sw_off + r0, TM), :] = jnp.where(lo, ksw_, k).astype(BF16)
        vsw[0, pl.ds(sw_off + r0, TM), :] = jnp.where(lo, v, vsw_).astype(BF16)
        vsw[1, pl.ds(sw_off + r0, TM), :] = jnp.where(lo, vsw_, v).astype(BF16)

        if latent:
            @pl.when(s == 0)
            def _first():
                cna = cna_ref[0]
                for p in range(NA_HEADS // 2):
                    kna[p, n:n + PAST_LEN, :] = cna[:, LANES * p:LANES * (p + 1)]
                    vna[p, n:n + PAST_LEN, :] = cna[:, 512 + LANES * p:512 + LANES * (p + 1)]
                csw = csw_ref[0]
                zpad = jnp.zeros((SWA_WINDOW, LANES), BF16)
                for kv in range(SWA_KV_HEADS):
                    ksw[kv, 0:SWA_WINDOW, :] = zpad
                    vsw[kv, 0:SWA_WINDOW, :] = zpad
                    ksw[kv, n + SWA_WINDOW:n + 2 * SWA_WINDOW, :] = zpad
                    vsw[kv, n + SWA_WINDOW:n + 2 * SWA_WINDOW, :] = zpad
                    ksw[kv, n + 2 * SWA_WINDOW:n + 2 * SWA_WINDOW + PAST_LEN, :] = csw[:, LANES * kv:LANES * (kv + 1)]
                    vsw[kv, n + 2 * SWA_WINDOW:n + 2 * SWA_WINDOW + PAST_LEN, :] = csw[:, 256 + LANES * kv:256 + LANES * (kv + 1)]

    @pl.when(s >= nt)
    def _attend():
        ti = s - nt
        q0 = pl.multiple_of(ti * TM, TM)
        rows_per_tile = TM // GRID_W
        nrows = n // GRID_W
        nloc = NA_KH * GRID_W

        def na_pair(p, carry):
            outs = []
            for a in (0, 1):
                hd = 2 * p + a
                qh = qna[hd, pl.ds(q0, TM), :]
                if latent:
                    s_loc, starts = [], []
                    for r in range(rows_per_tile):
                        row = ti * rows_per_tile + r
                        rs = jnp.clip(row - NA_KH // 2, 0, nrows - NA_KH)
                        k0 = pl.multiple_of(rs * GRID_W, GRID_W)
                        starts.append(k0)
                        s_loc.append(_dot_t(qh[GRID_W * r:GRID_W * (r + 1)], kna[p, pl.ds(k0, nloc), :])
                                     + bias_ref[hd, rs - row + NA_KH - 1])
                    s_loc = jnp.concatenate(s_loc, axis=0)
                    s_ctx = _dot_t(qh, kna[p, pl.ds(n, PAST_LEN), :])

                    def pv_loc(e):
                        return jnp.concatenate(
                            [_dot(e[GRID_W * r:GRID_W * (r + 1)], vna[p, pl.ds(starts[r], nloc), :])
                             for r in range(rows_per_tile)], axis=0)

                    outs.append(_softmax_pv([(s_loc, pv_loc), (s_ctx, vna[p, pl.ds(n, PAST_LEN), :])]))
                else:
                    outs.append(_softmax_pv([(_dot_t(qh, kna[p]), vna[p])]))
            osc[p] = jnp.where(lo, outs[0], outs[1]).astype(BF16)
            return carry

        lax.fori_loop(0, NA_HEADS // 2, na_pair, 0)

        group = SWA_HEADS // SWA_KV_HEADS
        for kv in range(SWA_KV_HEADS):
            qst = jnp.concatenate([qsw[group * kv + g, pl.ds(q0, TM), :] for g in range(group)], axis=0)
            sink = jnp.concatenate(
                [jnp.full((TM, 1), sink_ref[group * kv + g], F32) for g in range(group)], axis=0)
            if latent:
                span = TM + 2 * SWA_WINDOW
                s_loc = _dot_t(qst, ksw[kv, pl.ds(q0, span), :])
                a_i = lax.broadcasted_iota(jnp.int32, (TM, span), 0)
                b_i = lax.broadcasted_iota(jnp.int32, (TM, span), 1)
                key = q0 - SWA_WINDOW + b_i
                ok = (b_i >= a_i) & (b_i <= a_i + 2 * SWA_WINDOW) & (key >= 0) & (key < n)
                s_loc = jnp.concatenate(
                    [jnp.where(ok, s_loc[TM * g:TM * (g + 1)], NEG) for g in range(group)], axis=0)
                c0 = n + 2 * SWA_WINDOW
                s_ctx = _dot_t(qst, ksw[kv, pl.ds(c0, PAST_LEN), :])
                o = _softmax_pv([(s_loc, vsw[kv, pl.ds(q0, span), :]),
                                 (s_ctx, vsw[kv, pl.ds(c0, PAST_LEN), :])], extra=sink)
            else:
                o = _softmax_pv([(_dot_t(qst, ksw[kv]), vsw[kv])], extra=sink)
            for j in range(group // 2):
                osc[NA_HEADS // 2 + (group // 2) * kv + j] = jnp.where(
                    lo, o[TM * 2 * j:TM * (2 * j + 1)], o[TM * (2 * j + 1):TM * (2 * j + 2)]).astype(BF16)

        mix = jnp.concatenate([osc[p] for p in range(8)], axis=1)
        o_ref[...] = x_ref[...] + m[2:3] * _dot(mix, wout_ref[...])


def _odd_mixer(x, mod_l, n, gbase, latent, g_mix, w, tables, bias, cna, csw):
    t = x.shape[0]
    nb = t // n
    nt = n // TM
    nk = n + (PAST_LEN if latent else 0)
    nsw = n + (2 * SWA_WINDOW + PAST_LEN if latent else 0)

    def xmap(b, s):
        return (b * nt + s % nt, 0)

    def gmap(b, s):
        return (gbase + (b if latent else 0), 0, 0)

    in_specs = [
        pl.BlockSpec((TM, D_MODEL), xmap),
        pl.BlockSpec((1, N_MOD, D_MODEL), gmap),
        _const_spec((1, D_MODEL)),
        _const_spec(w["w_in"].shape),
        pl.BlockSpec(memory_space=pltpu.SMEM),
        _const_spec(w["w_out"].shape),
    ]
    args = [x, mod_l, g_mix, w["w_in"], w["sink"], w["w_out"]]
    out_shape = [jax.ShapeDtypeStruct((t, D_MODEL), F32)]
    out_specs = [pl.BlockSpec((TM, D_MODEL), lambda b, s: (b * nt + jnp.maximum(s - nt, 0), 0))]
    if latent:
        tspec = pl.BlockSpec((TM, LANES), lambda b, s: (jnp.minimum(s, nt - 1), 0))
        in_specs += [tspec, tspec, tspec, _const_spec(bias.shape),
                     pl.BlockSpec((1, PAST_LEN, 1024), lambda b, s: (b, 0, 0)),
                     pl.BlockSpec((1, PAST_LEN, 512), lambda b, s: (b, 0, 0))]
        args += [tables["cos"], tables["sin_prev"], tables["sin_next"], bias, cna, csw]
    else:
        def kvmap(b, s):
            return (b * nt + jnp.minimum(s, nt - 1), 0)
        out_shape += [jax.ShapeDtypeStruct((t, 1024), F32), jax.ShapeDtypeStruct((t, 256), F32)]
        out_specs += [pl.BlockSpec((TM, 1024), kvmap), pl.BlockSpec((TM, 256), kvmap)]
    scratch = [
        pltpu.VMEM((NA_HEADS, n, LANES), BF16),
        pltpu.VMEM((NA_HEADS // 2, nk, LANES), BF16),
        pltpu.VMEM((NA_HEADS // 2, nk, LANES), BF16),
        pltpu.VMEM((SWA_HEADS, n, LANES), BF16),
        pltpu.VMEM((SWA_KV_HEADS, nsw, LANES), BF16),
        pltpu.VMEM((SWA_KV_HEADS, nsw, LANES), BF16),
        pltpu.VMEM((8, TM, LANES), BF16),
    ]
    return pl.pallas_call(
        functools.partial(_odd_body, n=n, latent=latent),
        out_shape=out_shape,
        grid=(nb, 2 * nt),
        in_specs=in_specs,
        out_specs=out_specs,
        scratch_shapes=scratch,
        compiler_params=_cparams(("arbitrary", "arbitrary")),
        name="odd_latent" if latent else "odd_context",
    )(*args)


def _rope_tables(n, rot, lane0, period):
    half = rot // 2
    nf = half // 2
    t = jnp.arange(n)
    freqs = ROPE_BASE ** (-jnp.arange(nf, dtype=F32) / nf)
    cos = jnp.ones((n, period), F32)
    sp = jnp.zeros((n, period), F32)
    sn = jnp.zeros((n, period), F32)
    for k, pos in enumerate((t // GRID_W, t % GRID_W)):
        ang = pos.astype(F32)[:, None] * freqs
        c, s_ = jnp.cos(ang), jnp.sin(ang)
        a = lane0 + k * half
        cos = cos.at[:, a:a + nf].set(c).at[:, a + nf:a + half].set(c)
        sn = sn.at[:, a:a + nf].set(-s_)
        sp = sp.at[:, a + nf:a + half].set(s_)
    reps = LANES // period
    return {"cos": jnp.tile(cos, (1, reps)), "sin_prev": jnp.tile(sp, (1, reps)),
            "sin_next": jnp.tile(sn, (1, reps))}


def _pool_counts(n):
    t = np.arange(n)
    cols = []
    for wdw in POOL_WINDOWS:
        lo = np.clip(t - wdw // 2, 0, n)
        hi = np.clip(t + wdw // 2, 0, n)
        cols.append(np.repeat((hi - lo).astype(np.float32)[:, None], POOL_GROUP, axis=1))
    return jnp.asarray(np.concatenate(cols, axis=1))


def _even_weights(w_in, g_q, g_kv, w_uq, w_ukv, w_pool, pool_scale, w_out):
    a, b_, c_ = Q_LORA, Q_LORA + KV_LORA, Q_LORA + KV_LORA + QK_ROPE
    krpad = jnp.zeros((D_MODEL, LANES), F32).at[:, 64:96].set(w_in[:, b_:c_])
    w_in_p = jnp.concatenate([w_in[:, :b_], w_in[:, c_:], krpad], axis=1)
    uq = w_uq.reshape(Q_LORA, MLA_HEADS, QK_NOPE + QK_ROPE)
    uq = jnp.pad(uq, ((0, 0), (0, 0), (0, LANES - QK_NOPE - QK_ROPE))).reshape(Q_LORA, MLA_HEADS * LANES)
    ukv = w_ukv.reshape(KV_LORA, MLA_HEADS, QK_NOPE + V_DIM)
    uk = jnp.pad(ukv[..., :QK_NOPE], ((0, 0), (0, 0), (0, LANES - QK_NOPE)))
    sel = jnp.zeros((LANES, MLA_HEADS, LANES), F32)
    sel = sel.at[64 + jnp.arange(QK_ROPE), :, 64 + jnp.arange(QK_ROPE)].set(1.0)
    w_kcat = jnp.concatenate([uk, sel], axis=0).reshape(KV_LORA + LANES, MLA_HEADS * LANES)
    w_uv = ukv[..., QK_NOPE:].reshape(KV_LORA, MLA_HEADS * V_DIM)
    w_pool_bd = jax.scipy.linalg.block_diag(*[w_pool[g] for g in range(4)])
    return {"w_in": w_in_p.astype(BF16), "g_q": g_q.reshape(1, -1), "g_kv": g_kv.reshape(1, -1),
            "w_uq": uq.astype(BF16), "w_kcat": w_kcat.astype(BF16), "w_uv": w_uv.astype(BF16),
            "w_pool": w_pool_bd.astype(BF16), "pool_scale": pool_scale.reshape(1, -1),
            "w_out": w_out.astype(BF16)}


def _ffn_weights(w_up, conv_w, conv_b, w_down):
    def chunks(a):
        lead = a.shape[:-1]
        a = a.reshape(lead + (2, N_FF_CHUNKS, FF_CHUNK))
        a = jnp.moveaxis(a, -2, 0)
        return a.reshape((N_FF_CHUNKS,) + lead + (2 * FF_CHUNK,))
    return (chunks(w_up).astype(BF16), chunks(conv_w), chunks(conv_b.reshape(1, -1)),
            w_down.reshape(N_FF_CHUNKS, FF_CHUNK, D_MODEL).astype(BF16))


def kernel(x_prompt, x_sample, cache_mla_latent, cache_na_kv, cache_swa_kv, c, c_ctx, w_mod, b_mod, norm_mix, norm_ffn, norm_final, w_in_even, mla_q_norm, mla_kv_norm, w_uq, w_ukv, w_pool, pool_scale, w_out_even, w_in_odd, na_rpb, swa_sink, w_out_odd, w_up, conv_w, conv_b, w_down):
    nbp, n_p, _ = x_prompt.shape
    nbs, n_s, _ = x_sample.shape
    depth = w_mod.shape[0]
    xp = x_prompt.reshape(nbp * n_p, D_MODEL)
    xs = x_sample.reshape(nbs * n_s, D_MODEL)

    c_all = jnp.zeros((8, D_MODEL), F32).at[0].set(c_ctx).at[1:1 + nbs].set(c)
    mod = _adaln(c_all, w_mod, b_mod).reshape(depth, 8, N_MOD, D_MODEL)

    tab_mla = _rope_tables(n_s, QK_ROPE, 64, LANES)
    tab_swa = _rope_tables(n_s, HEAD_DIM, 0, HEAD_DIM)
    cnt_p, cnt_s = _pool_counts(n_p), _pool_counts(n_s)

    nh = na_rpb.shape[1]
    blocks = _na_bias_blocks(na_rpb.reshape((-1,) + na_rpb.shape[2:]))
    bias_all = jnp.stack([jnp.concatenate([blocks[:, v + i] for i in range(NA_KH)], axis=-1)
                          for v in range(NA_KH)], axis=1).reshape(-1, nh, NA_KH, GRID_W, NA_KH * GRID_W)

    lats, nakvs, swkvs = [], [], []
    for l in range(depth):
        e = l // 2
        mod_l = mod[l]
        g_mix = norm_mix[l].reshape(1, -1)
        if l % 2 == 0:
            w = _even_weights(w_in_even[e], mla_q_norm[e], mla_kv_norm[e], w_uq[e], w_ukv[e], w_pool[e],
                              pool_scale[e], w_out_even[e])
            ctx = cache_mla_latent[:, e]
            ctx = jnp.concatenate([ctx[..., :KV_LORA], jnp.zeros(ctx.shape[:2] + (64,), F32),
                                   ctx[..., KV_LORA:], jnp.zeros(ctx.shape[:2] + (32,), F32)],
                                  axis=-1).astype(BF16)
            xp, lat = _even_mixer(xp, mod_l, n_p, 0, False, g_mix, w, {"cnt": cnt_p}, None)
            (xs,) = _even_mixer(xs, mod_l, n_s, 1, True, g_mix, w, dict(tab_mla, cnt=cnt_s), ctx)
            lats.append(jnp.concatenate([lat[:, :KV_LORA], lat[:, KV_LORA + 64:KV_LORA + 96]], axis=-1)
                        .reshape(nbp, n_p, KV_LORA + QK_ROPE))
        else:
            w = {"w_in": w_in_odd[e].astype(BF16), "w_out": w_out_odd[e].astype(BF16), "sink": swa_sink[e]}
            cna = cache_na_kv[:, e].reshape(nbs, PAST_LEN, 2 * NA_HEADS * HEAD_DIM).astype(BF16)
            csw = cache_swa_kv[:, e].reshape(nbs, PAST_LEN, 2 * SWA_KV_HEADS, HEAD_DIM)
            csw = jnp.concatenate([csw, csw], axis=-1).reshape(nbs, PAST_LEN, 512).astype(BF16)
            xp, nakv, swkv = _odd_mixer(xp, mod_l, n_p, 0, False, g_mix, w, None, None, None, None)
            (xs,) = _odd_mixer(xs, mod_l, n_s, 1, True, g_mix, w, tab_swa, bias_all[e], cna, csw)
            nakvs.append(nakv.reshape(nbp, n_p, 2, NA_HEADS, HEAD_DIM))
            swkvs.append(swkv.reshape(nbp, n_p, 2, SWA_KV_HEADS, HEAD_DIM))
        wup, cw, cb, wdn = _ffn_weights(w_up[l], conv_w[l], conv_b[l], w_down[l])
        g_ffn = norm_ffn[l].reshape(1, -1)
        g_fin = norm_final.reshape(1, -1)
        final = l == depth - 1
        xp = _ffn(xp, mod_l, n_p, 0, False, g_ffn, wup, cw, cb, wdn, g_fin, final)
        xs = _ffn(xs, mod_l, n_s, 1, True, g_ffn, wup, cw, cb, wdn, g_fin, final)

    return (xp.reshape(nbp, n_p, D_MODEL), xs.reshape(nbs, n_s, D_MODEL),
            jnp.stack(lats, axis=1), jnp.stack(nakvs, axis=1), jnp.stack(swkvs, axis=1))
```

```python
import functools

import numpy as np
import jax
import jax.numpy as jnp
from jax import lax
from jax.experimental import pallas as pl
from jax.experimental.pallas import tpu as pltpu

F32 = jnp.float32
BF16 = jnp.bfloat16

D_MODEL = 1024
GRID_W = 64
N_MOD = 6
EPS = 1e-6
ROPE_BASE = 10000.0
NEG = -1e30
MLA_HEADS = 12
Q_LORA = 384
KV_LORA = 256
QK_NOPE = 64
QK_ROPE = 32
V_DIM = 64
MLA_SCALE = (QK_NOPE + QK_ROPE) ** -0.5
LOG2E = 1.4426950408889634
POOL_WINDOWS = (2, 4, 8, 16)
POOL_GROUP = 64
POOL_DIM = 256
NA_HEADS = 8
NA_KH = 8
NA_KW = 16
SWA_HEADS = 8
SWA_KV_HEADS = 2
SWA_WINDOW = 128
HEAD_DIM = 64
D_FF = 2816
PAST_LEN = 512

LANES = 128
TM = 256
FF_CHUNK = 256
N_FF_CHUNKS = D_FF // FF_CHUNK
POOL_PAD = 16
VMEM_LIMIT = 60000 * 1024


def _cparams(sem):
    return pltpu.CompilerParams(dimension_semantics=sem, vmem_limit_bytes=VMEM_LIMIT)


def _rms(x, g):
    return x * lax.rsqrt(jnp.mean(x * x, axis=-1, keepdims=True) + EPS) * g


def _silu(x):
    return x * (1.0 / (1.0 + jnp.exp(-x)))


def _dot(a, b):
    return jnp.dot(a, b, preferred_element_type=F32)


def _dot_t(a, b):
    return lax.dot_general(a, b, (((1,), (1,)), ((), ())), preferred_element_type=F32)


def _rope(x, cos, sin_prev, sin_next, shift):
    w = x.shape[-1]
    return x * cos + pltpu.roll(x, shift, 1) * sin_prev + pltpu.roll(x, w - shift, 1) * sin_next


def _const_spec(shape):
    nd = len(shape)
    return pl.BlockSpec(shape, lambda *_: (0,) * nd, pipeline_mode=pl.Buffered(1))


def _adaln_body(c_ref, w_ref, b_ref, o_ref):
    a = _silu(c_ref[...]).astype(BF16)
    o_ref[0] = _dot(a, w_ref[0].astype(BF16)) + b_ref[0]


def _adaln(c_all, w_mod, b_mod):
    depth, _, width = w_mod.shape
    tn = 1536
    return pl.pallas_call(
        _adaln_body,
        out_shape=jax.ShapeDtypeStruct((depth, 8, width), F32),
        grid=(depth, width // tn),
        in_specs=[
            pl.BlockSpec((8, D_MODEL), lambda l, j: (0, 0)),
            pl.BlockSpec((1, D_MODEL, tn), lambda l, j: (l, 0, j)),
            pl.BlockSpec((1, 1, tn), lambda l, j: (l, 0, j)),
        ],
        out_specs=pl.BlockSpec((1, 8, tn), lambda l, j: (l, 0, j)),
        compiler_params=_cparams(("arbitrary", "arbitrary")),
        name="adaln",
    )(c_all, w_mod, b_mod.reshape(depth, 1, width))


def _ffn_body(x_ref, xp_ref, xn_ref, mod_ref, g_ref, wup_ref, cw_ref, cb_ref, wdn_ref, gf_ref,
              o_ref, act_ref, *, n, final):
    i = pl.program_id(0)
    m = mod_ref[0]
    shift, scale, gate = m[3:4], m[4:5], m[5:6]
    g = g_ref[...]

    def hn(x):
        return _rms(x, g) * (1.0 + scale) + shift

    x = x_ref[...]
    tile_in_seq = i % (n // TM)
    keep_prev = jnp.where(tile_in_seq == 0, 0.0, 1.0)
    keep_next = jnp.where(tile_in_seq == n // TM - 1, 0.0, 1.0)
    hext = jnp.concatenate([hn(xp_ref[...]) * keep_prev, hn(x), hn(xn_ref[...]) * keep_next],
                           axis=0).astype(BF16)

    def conv(c0):
        u = _dot(hext, wup_ref[:, c0:c0 + FF_CHUNK])
        w = cw_ref[:, c0:c0 + FF_CHUNK]
        return (pltpu.roll(u, 1, 0)[8:8 + TM] * w[0:1] + u[8:8 + TM] * w[1:2]
                + pltpu.roll(u, TM + 15, 0)[8:8 + TM] * w[2:3] + cb_ref[:, c0:c0 + FF_CHUNK])

    for j in range(N_FF_CHUNKS):
        act_ref[:, j * FF_CHUNK:(j + 1) * FF_CHUNK] = (
            _silu(conv(D_FF + j * FF_CHUNK)) * conv(j * FF_CHUNK)).astype(BF16)
    y = x + gate * _dot(act_ref[...], wdn_ref[...])
    if final:
        y = _rms(y, gf_ref[...])
    o_ref[...] = y


def _ffn(x, mod_l, n, gbase, per_seq, g_ffn, wup, cw, cb, wdn, g_final, final):
    t = x.shape[0]
    nblk8 = t // 8
    tiles_per_seq = n // TM if per_seq else 0

    def gmap(i):
        return (gbase + (i // tiles_per_seq if per_seq else 0), 0, 0)

    return pl.pallas_call(
        functools.partial(_ffn_body, n=n, final=final),
        out_shape=jax.ShapeDtypeStruct((t, D_MODEL), F32),
        grid=(t // TM,),
        in_specs=[
            pl.BlockSpec((TM, D_MODEL), lambda i: (i, 0)),
            pl.BlockSpec((8, D_MODEL), lambda i: (jnp.maximum(i * (TM // 8) - 1, 0), 0)),
            pl.BlockSpec((8, D_MODEL), lambda i: (jnp.minimum((i + 1) * (TM // 8), nblk8 - 1), 0)),
            pl.BlockSpec((1, N_MOD, D_MODEL), gmap),
            _const_spec((1, D_MODEL)),
            _const_spec(wup.shape),
            _const_spec(cw.shape),
            _const_spec(cb.shape),
            _const_spec(wdn.shape),
            _const_spec((1, D_MODEL)),
        ],
        out_specs=pl.BlockSpec((TM, D_MODEL), lambda i: (i, 0)),
        scratch_shapes=[pltpu.VMEM((TM, D_FF), BF16)],
        compiler_params=_cparams(("arbitrary",)),
        name="conv_ffn",
    )(x, x, x, mod_l, g_ffn, wup, cw, cb, wdn, g_final)


def _state_specs(nb, n_slots, n, width, slot, prev, tile_of):
    shape = jax.ShapeDtypeStruct((nb, n_slots, n, width), F32)
    if prev is None:
        return shape, pl.BlockSpec((1, n_slots, TM, width), lambda b, s: (b, 0, tile_of(s), 0))
    return shape, pl.BlockSpec((1, 1, TM, width), lambda b, s: (b, slot, tile_of(s), 0))


def _write_state(ref, slot, val):
    if ref.shape[1] == 1:
        ref[0, 0] = val
    else:
        for k in range(ref.shape[1]):
            ref[0, k] = val if k == slot else jnp.zeros_like(val)


def _even_body(*refs, n, latent, slot, heads_per_step):
    if latent:
        (x_ref, mod_ref, g_ref, win_ref, gq_ref, gkv_ref, wuq_ref, wkc_ref, wuv_ref, wpool_ref,
         pscale_ref, cnt_ref, wout_ref, cos_ref, sp_ref, sn_ref, ctx_ref,
         o_ref, qs, ks, vs, xps, osc) = refs
        lat_ref = None
    else:
        (x_ref, mod_ref, g_ref, win_ref, gq_ref, gkv_ref, wuq_ref, wkc_ref, wuv_ref, wpool_ref,
         pscale_ref, cnt_ref, wout_ref) = refs[:13]
        o_ref, lat_ref, qs, ks, vs, xps, osc = refs[-7:]
    nt = n // TM
    s = pl.program_id(1)
    m = mod_ref[0]
    lane = lax.broadcasted_iota(jnp.int32, (TM, LANES), 1)
    lo = lane < V_DIM

    def store_v(v, rows):
        for p in range(MLA_HEADS // 2):
            vp = v[:, LANES * p:LANES * (p + 1)]
            ln = lane[:v.shape[0]]
            vs[2 * p, rows, :] = jnp.where(ln < V_DIM, vp, jnp.where(ln == V_DIM, 1.0, 0.0)).astype(BF16)
            vs[2 * p + 1, rows, :] = jnp.where(ln >= V_DIM, vp, jnp.where(ln == 0, 1.0, 0.0)).astype(BF16)

    @pl.when(s < nt)
    def _project():
        r0 = pl.multiple_of(s * TM, TM)
        h = (_rms(x_ref[...], g_ref[...]) * (1.0 + m[1:2]) + m[0:1]).astype(BF16)
        z = _dot(h, win_ref[...])
        qn = _rms(z[:, :Q_LORA], gq_ref[...]).astype(BF16)
        q = _dot(qn, wuq_ref[...])
        latc = _rms(z[:, Q_LORA:Q_LORA + KV_LORA], gkv_ref[...])
        krp = z[:, 896:1024]
        if latent:
            cos, sp, sn = cos_ref[...], sp_ref[...], sn_ref[...]
            krp = _rope(krp, cos, sp, sn, 8)
        if lat_ref is not None:
            kr0 = pltpu.roll(krp, LANES - 64, 1)
            _write_state(lat_ref, slot, jnp.concatenate([latc, kr0[:, :QK_ROPE]], axis=1))
        lb = jnp.concatenate([latc, krp], axis=1).astype(BF16)
        kc = _dot(lb, wkc_ref[...])
        v = _dot(lb[:, :KV_LORA], wuv_ref[...])
        for hd in range(MLA_HEADS):
            qh = q[:, LANES * hd:LANES * (hd + 1)]
            if latent:
                qh = _rope(qh, cos, sp, sn, 8)
            qs[hd, pl.ds(r0, TM), :] = (qh * (MLA_SCALE * LOG2E)).astype(BF16)
            ks[hd, pl.ds(r0, TM), :] = kc[:, LANES * hd:LANES * (hd + 1)].astype(BF16)
        store_v(v, pl.ds(r0, TM))
        xps[pl.ds(POOL_PAD + r0, TM), :] = z[:, 640:896]

        @pl.when(s == 0)
        def _first():
            xps[0:POOL_PAD, :] = jnp.zeros((POOL_PAD, POOL_DIM), F32)
            xps[n + POOL_PAD:n + 2 * POOL_PAD, :] = jnp.zeros((POOL_PAD, POOL_DIM), F32)
            if latent:
                cb = ctx_ref[0]
                kcc = _dot(cb, wkc_ref[...])
                for hd in range(MLA_HEADS):
                    ks[hd, n:n + PAST_LEN, :] = kcc[:, LANES * hd:LANES * (hd + 1)].astype(BF16)
                for half in range(PAST_LEN // TM):
                    vc = _dot(cb[TM * half:TM * (half + 1), :KV_LORA], wuv_ref[...])
                    store_v(vc, pl.ds(n + TM * half, TM))

    @pl.when(s >= nt)
    def _attend():
        q0 = pl.multiple_of((s - nt) * TM, TM)

        def head(hd, l_lane):
            sc = _dot_t(qs[hd, pl.ds(q0, TM), :], ks[hd])
            e = jnp.exp2(sc - jnp.max(sc, axis=-1, keepdims=True))
            o = _dot(e.astype(BF16), vs[hd])
            return o * (1.0 / o[:, l_lane:l_lane + 1])

        def pair(p):
            osc[p] = jnp.where(lo, head(2 * p, V_DIM), head(2 * p + 1, 0)).astype(BF16)

        pairs_per_step = heads_per_step // 2
        if pairs_per_step == MLA_HEADS // 2:
            for p in range(MLA_HEADS // 2):
                pair(p)
        else:
            def step(i, carry):
                for j in range(pairs_per_step):
                    pair(i * pairs_per_step + j)
                return carry
            lax.fori_loop(0, MLA_HEADS // 2 // pairs_per_step, step, 0)

        rows = TM + 2 * POOL_PAD
        a0 = xps[pl.ds(q0, rows), :]
        s1 = a0 + pltpu.roll(a0, 1, 0)
        s2 = s1 + pltpu.roll(s1, 2, 0)
        s4 = s2 + pltpu.roll(s2, 4, 0)
        s8 = s4 + pltpu.roll(s4, 8, 0)
        pl_lane = lax.broadcasted_iota(jnp.int32, (rows, POOL_DIM), 1)
        win = jnp.where(pl_lane < 64, s1,
                        jnp.where(pl_lane < 128, pltpu.roll(s2, rows - 1, 0),
                                  jnp.where(pl_lane < 192, pltpu.roll(s4, rows - 3, 0),
                                            pltpu.roll(s8, rows - 7, 0))))
        pooled = win[POOL_PAD:POOL_PAD + TM] / cnt_ref[...] - a0[POOL_PAD:POOL_PAD + TM]
        ypool = _dot(pooled.astype(BF16), wpool_ref[...]) * pscale_ref[...]

        mix = jnp.concatenate([osc[p] for p in range(MLA_HEADS // 2)] + [ypool.astype(BF16)], axis=1)
        o_ref[...] = x_ref[...] + m[2:3] * _dot(mix, wout_ref[...])


def _even_mixer(x, mod_l, n, gbase, latent, g_mix, w, tables, ctx, slot=0, n_slots=1, prev_lat=None):
    t = x.shape[0]
    nb = t // n
    nt = n // TM
    nk = n + (PAST_LEN if latent else 0)

    def xmap(b, s):
        return (b * nt + s % nt, 0)

    def gmap(b, s):
        return (gbase + (b if latent else 0), 0, 0)

    in_specs = [
        pl.BlockSpec((TM, D_MODEL), xmap),
        pl.BlockSpec((1, N_MOD, D_MODEL), gmap),
        _const_spec((1, D_MODEL)),
        _const_spec(w["w_in"].shape),
        _const_spec((1, Q_LORA)),
        _const_spec((1, KV_LORA)),
        _const_spec(w["w_uq"].shape),
        _const_spec(w["w_kcat"].shape),
        _const_spec(w["w_uv"].shape),
        _const_spec(w["w_pool"].shape),
        _const_spec((1, POOL_DIM)),
        pl.BlockSpec((TM, POOL_DIM), lambda b, s: (jnp.maximum(s - nt, 0), 0)),
        _const_spec(w["w_out"].shape),
    ]
    args = [x, mod_l, g_mix, w["w_in"], w["g_q"], w["g_kv"], w["w_uq"], w["w_kcat"], w["w_uv"],
            w["w_pool"], w["pool_scale"], tables["cnt"], w["w_out"]]
    out_shape = [jax.ShapeDtypeStruct((t, D_MODEL), F32)]
    out_specs = [pl.BlockSpec((TM, D_MODEL), lambda b, s: (b * nt + jnp.maximum(s - nt, 0), 0))]
    aliases = {}
    if latent:
        tspec = pl.BlockSpec((TM, LANES), lambda b, s: (jnp.minimum(s, nt - 1), 0))
        in_specs += [tspec, tspec, tspec, pl.BlockSpec((1, PAST_LEN, 384), lambda b, s: (b, 0, 0))]
        args += [tables["cos"], tables["sin_prev"], tables["sin_next"], ctx]
    else:
        shape, spec = _state_specs(nb, n_slots, n, KV_LORA + QK_ROPE, slot, prev_lat,
                                   lambda s: jnp.minimum(s, nt - 1))
        out_shape.append(shape)
        out_specs.append(spec)
        if prev_lat is not None:
            aliases = {len(args): 1}
            in_specs.append(pl.BlockSpec(memory_space=pl.ANY))
            args.append(prev_lat)
    scratch = [
        pltpu.VMEM((MLA_HEADS, n, LANES), BF16),
        pltpu.VMEM((MLA_HEADS, nk, LANES), BF16),
        pltpu.VMEM((MLA_HEADS, nk, LANES), BF16),
        pltpu.VMEM((n + 2 * POOL_PAD, POOL_DIM), F32),
        pltpu.VMEM((MLA_HEADS // 2, TM, LANES), BF16),
    ]
    return pl.pallas_call(
        functools.partial(_even_body, n=n, latent=latent, slot=slot,
                          heads_per_step=MLA_HEADS),
        out_shape=out_shape,
        grid=(nb, 2 * nt),
        in_specs=in_specs,
        out_specs=out_specs,
        scratch_shapes=scratch,
        input_output_aliases=aliases,
        compiler_params=_cparams(("arbitrary", "arbitrary")),
        name="even_latent" if latent else "even_context",
    )(*args)


def _bias_body(rpb_ref, o_ref):
    h = pl.program_id(0)
    qc = lax.broadcasted_iota(jnp.int32, (GRID_W, GRID_W), 0)
    kc = lax.broadcasted_iota(jnp.int32, (GRID_W, GRID_W), 1)
    c0 = jnp.clip(qc - NA_KW // 2, 0, GRID_W - NA_KW)
    inwin = (kc >= c0) & (kc < c0 + NA_KW)
    dc = kc - qc + NA_KW - 1
    nj = 2 * NA_KW - 1

    def per_dr(dr, carry):
        acc = jnp.full((GRID_W, GRID_W), NEG, F32)
        for j in range(nj):
            acc = jnp.where(inwin & (dc == j), rpb_ref[(h * (2 * NA_KH - 1) + dr) * nj + j], acc)
        o_ref[0, dr] = acc
        return carry

    lax.fori_loop(0, 2 * NA_KH - 1, per_dr, 0)


def _na_bias_blocks(rpb):
    nh, ndr, nj = rpb.shape
    return pl.pallas_call(
        _bias_body,
        out_shape=jax.ShapeDtypeStruct((nh, ndr, GRID_W, GRID_W), F32),
        grid=(nh,),
        in_specs=[pl.BlockSpec(memory_space=pltpu.SMEM)],
        out_specs=pl.BlockSpec((1, ndr, GRID_W, GRID_W), lambda h: (h, 0, 0, 0)),
        compiler_params=_cparams(("arbitrary",)),
        name="na_bias",
    )(rpb.reshape(-1))


def _softmax_pv(parts, extra=None):
    mx = functools.reduce(jnp.maximum, [jnp.max(sc, axis=-1, keepdims=True) for sc, _ in parts])
    if extra is not None:
        mx = jnp.maximum(mx, extra)
    l = jnp.exp(extra - mx) if extra is not None else 0.0
    acc = None
    for sc, val in parts:
        e = jnp.exp(sc - mx)
        l = l + jnp.sum(e, axis=-1, keepdims=True)
        o = val(e.astype(BF16)) if callable(val) else _dot(e.astype(BF16), val)
        acc = o if acc is None else acc + o
    return acc / l


def _odd_body(*refs, n, latent, slot):
    if latent:
        (x_ref, mod_ref, g_ref, win_ref, sink_ref, wout_ref, cos_ref, sp_ref, sn_ref, bias_ref,
         cna_ref, csw_ref,
         o_ref, qna, kna, vna, qsw, ksw, vsw, osc) = refs
        nakv_ref = swkv_ref = None
    else:
        x_ref, mod_ref, g_ref, win_ref, sink_ref, wout_ref = refs[:6]
        o_ref, nakv_ref, swkv_ref, qna, kna, vna, qsw, ksw, vsw, osc = refs[-10:]
    nt = n // TM
    s = pl.program_id(1)
    m = mod_ref[0]
    sw_off = SWA_WINDOW if latent else 0
    lane = lax.broadcasted_iota(jnp.int32, (TM, LANES), 1)
    lo = lane < HEAD_DIM

    @pl.when(s < nt)
    def _project():
        r0 = pl.multiple_of(s * TM, TM)
        h = (_rms(x_ref[...], g_ref[...]) * (1.0 + m[1:2]) + m[0:1]).astype(BF16)
        z = _dot(h, win_ref[...])
        if nakv_ref is not None:
            _write_state(nakv_ref, slot, z[:, 512:1536])
            _write_state(swkv_ref, slot, z[:, 2048:2304])
        scale = HEAD_DIM ** -0.5
        for p in range(NA_HEADS // 2):
            qp = z[:, LANES * p:LANES * (p + 1)] * scale
            qna[2 * p, pl.ds(r0, TM), :] = jnp.where(lo, qp, 0.0).astype(BF16)
            qna[2 * p + 1, pl.ds(r0, TM), :] = jnp.where(lo, 0.0, qp).astype(BF16)
            kna[p, pl.ds(r0, TM), :] = z[:, 512 + LANES * p:512 + LANES * (p + 1)].astype(BF16)
            vna[p, pl.ds(r0, TM), :] = z[:, 1024 + LANES * p:1024 + LANES * (p + 1)].astype(BF16)
        if latent:
            cos, sp, sn = cos_ref[...], sp_ref[...], sn_ref[...]
        for p in range(SWA_HEADS // 2):
            qp = z[:, 1536 + LANES * p:1536 + LANES * (p + 1)]
            if latent:
                qp = _rope(qp, cos, sp, sn, 16)
            qp = qp * scale
            qsw[2 * p, pl.ds(r0, TM), :] = jnp.where(lo, qp, 0.0).astype(BF16)
            qsw[2 * p + 1, pl.ds(r0, TM), :] = jnp.where(lo, 0.0, qp).astype(BF16)
        k = z[:, 2048:2176]
        if latent:
            k = _rope(k, cos, sp, sn, 16)
        v = z[:, 2176:2304]
        ksw_, vsw_ = pltpu.roll(k, HEAD_DIM, 1), pltpu.roll(v, HEAD_DIM, 1)
        ksw[0, pl.ds(sw_off + r0, TM), :] = jnp.where(lo, k, ksw_).astype(BF16)
        ksw[1, pl.ds(sw_off + r0, TM), :] = jnp.where(lo, ksw_, k).astype(BF16)
        vsw[0, pl.ds(sw_off + r0, TM), :] = jnp.where(lo, v, vsw_).astype(BF16)
        vsw[1, pl.ds(sw_off + r0, TM), :] = jnp.where(lo, vsw_, v).astype(BF16)

        if latent:
            @pl.when(s == 0)
            def _first():
                cna = cna_ref[0]
                for p in range(NA_HEADS // 2):
                    kna[p, n:n + PAST_LEN, :] = cna[:, LANES * p:LANES * (p + 1)]
                    vna[p, n:n + PAST_LEN, :] = cna[:, 512 + LANES * p:512 + LANES * (p + 1)]
                csw = csw_ref[0]
                zpad = jnp.zeros((SWA_WINDOW, LANES), BF16)
                for kv in range(SWA_KV_HEADS):
                    ksw[kv, 0:SWA_WINDOW, :] = zpad
                    vsw[kv, 0:SWA_WINDOW, :] = zpad
                    ksw[kv, n + SWA_WINDOW:n + 2 * SWA_WINDOW, :] = zpad
                    vsw[kv, n + SWA_WINDOW:n + 2 * SWA_WINDOW, :] = zpad
                    ksw[kv, n + 2 * SWA_WINDOW:n + 2 * SWA_WINDOW + PAST_LEN, :] = csw[:, LANES * kv:LANES * (kv + 1)]
                    vsw[kv, n + 2 * SWA_WINDOW:n + 2 * SWA_WINDOW + PAST_LEN, :] = csw[:, 256 + LANES * kv:256 + LANES * (kv + 1)]

    @pl.when(s >= nt)
    def _attend():
        ti = s - nt
        q0 = pl.multiple_of(ti * TM, TM)
        rows_per_tile = TM // GRID_W
        nrows = n // GRID_W
        nloc = NA_KH * GRID_W

        def na_pair(p):
            outs = []
            for a in (0, 1):
                hd = 2 * p + a
                qh = qna[hd, pl.ds(q0, TM), :]
                if latent:
                    s_loc, starts = [], []
                    for r in range(rows_per_tile):
                        row = ti * rows_per_tile + r
                        rs = jnp.clip(row - NA_KH // 2, 0, nrows - NA_KH)
                        k0 = pl.multiple_of(rs * GRID_W, GRID_W)
                        starts.append(k0)
                        s_loc.append(_dot_t(qh[GRID_W * r:GRID_W * (r + 1)], kna[p, pl.ds(k0, nloc), :])
                                     + bias_ref[hd, rs - row + NA_KH - 1])
                    s_loc = jnp.concatenate(s_loc, axis=0)
                    s_ctx = _dot_t(qh, kna[p, pl.ds(n, PAST_LEN), :])

                    def pv_loc(e):
                        return jnp.concatenate(
                            [_dot(e[GRID_W * r:GRID_W * (r + 1)], vna[p, pl.ds(starts[r], nloc), :])
                             for r in range(rows_per_tile)], axis=0)

                    outs.append(_softmax_pv([(s_loc, pv_loc), (s_ctx, vna[p, pl.ds(n, PAST_LEN), :])]))
                else:
                    outs.append(_softmax_pv([(_dot_t(qh, kna[p]), vna[p])]))
            osc[p] = jnp.where(lo, outs[0], outs[1]).astype(BF16)

        if latent:
            def na_step(p, carry):
                na_pair(p)
                return carry
            lax.fori_loop(0, NA_HEADS // 2, na_step, 0)
        else:
            for p in range(NA_HEADS // 2):
                na_pair(p)

        group = SWA_HEADS // SWA_KV_HEADS
        for kv in range(SWA_KV_HEADS):
            qst = jnp.concatenate([qsw[group * kv + g, pl.ds(q0, TM), :] for g in range(group)], axis=0)
            sink = jnp.concatenate(
                [jnp.full((TM, 1), sink_ref[group * kv + g], F32) for g in range(group)], axis=0)
            if latent:
                span = TM + 2 * SWA_WINDOW
                s_loc = _dot_t(qst, ksw[kv, pl.ds(q0, span), :])
                a_i = lax.broadcasted_iota(jnp.int32, (TM, span), 0)
                b_i = lax.broadcasted_iota(jnp.int32, (TM, span), 1)
                key = q0 - SWA_WINDOW + b_i
                ok = (b_i >= a_i) & (b_i <= a_i + 2 * SWA_WINDOW) & (key >= 0) & (key < n)
                s_loc = jnp.concatenate(
                    [jnp.where(ok, s_loc[TM * g:TM * (g + 1)], NEG) for g in range(group)], axis=0)
                c0 = n + 2 * SWA_WINDOW
                s_ctx = _dot_t(qst, ksw[kv, pl.ds(c0, PAST_LEN), :])
                o = _softmax_pv([(s_loc, vsw[kv, pl.ds(q0, span), :]),
                                 (s_ctx, vsw[kv, pl.ds(c0, PAST_LEN), :])], extra=sink)
            else:
                o = _softmax_pv([(_dot_t(qst, ksw[kv]), vsw[kv])], extra=sink)
            for j in range(group // 2):
                osc[NA_HEADS // 2 + (group // 2) * kv + j] = jnp.where(
                    lo, o[TM * 2 * j:TM * (2 * j + 1)], o[TM * (2 * j + 1):TM * (2 * j + 2)]).astype(BF16)

        mix = jnp.concatenate([osc[p] for p in range(8)], axis=1)
        o_ref[...] = x_ref[...] + m[2:3] * _dot(mix, wout_ref[...])


def _odd_mixer(x, mod_l, n, gbase, latent, g_mix, w, tables, bias, cna, csw, slot=0, n_slots=1, prev=None):
    t = x.shape[0]
    nb = t // n
    nt = n // TM
    nk = n + (PAST_LEN if latent else 0)
    nsw = n + (2 * SWA_WINDOW + PAST_LEN if latent else 0)

    def xmap(b, s):
        return (b * nt + s % nt, 0)

    def gmap(b, s):
        return (gbase + (b if latent else 0), 0, 0)

    in_specs = [
        pl.BlockSpec((TM, D_MODEL), xmap),
        pl.BlockSpec((1, N_MOD, D_MODEL), gmap),
        _const_spec((1, D_MODEL)),
        _const_spec(w["w_in"].shape),
        pl.BlockSpec(memory_space=pltpu.SMEM),
        _const_spec(w["w_out"].shape),
    ]
    args = [x, mod_l, g_mix, w["w_in"], w["sink"], w["w_out"]]
    out_shape = [jax.ShapeDtypeStruct((t, D_MODEL), F32)]
    out_specs = [pl.BlockSpec((TM, D_MODEL), lambda b, s: (b * nt + jnp.maximum(s - nt, 0), 0))]
    aliases = {}
    if latent:
        tspec = pl.BlockSpec((TM, LANES), lambda b, s: (jnp.minimum(s, nt - 1), 0))
        in_specs += [tspec, tspec, tspec, _const_spec(bias.shape),
                     pl.BlockSpec((1, PAST_LEN, 1024), lambda b, s: (b, 0, 0)),
                     pl.BlockSpec((1, PAST_LEN, 512), lambda b, s: (b, 0, 0))]
        args += [tables["cos"], tables["sin_prev"], tables["sin_next"], bias, cna, csw]
    else:
        for k, width in enumerate((2 * NA_HEADS * HEAD_DIM, 2 * SWA_KV_HEADS * HEAD_DIM)):
            shape, spec = _state_specs(nb, n_slots, n, width, slot, prev, lambda s: jnp.minimum(s, nt - 1))
            out_shape.append(shape)
            out_specs.append(spec)
            if prev is not None:
                aliases[len(args)] = 1 + k
                in_specs.append(pl.BlockSpec(memory_space=pl.ANY))
                args.append(prev[k])
    scratch = [
        pltpu.VMEM((NA_HEADS, n, LANES), BF16),
        pltpu.VMEM((NA_HEADS // 2, nk, LANES), BF16),
        pltpu.VMEM((NA_HEADS // 2, nk, LANES), BF16),
        pltpu.VMEM((SWA_HEADS, n, LANES), BF16),
        pltpu.VMEM((SWA_KV_HEADS, nsw, LANES), BF16),
        pltpu.VMEM((SWA_KV_HEADS, nsw, LANES), BF16),
        pltpu.VMEM((8, TM, LANES), BF16),
    ]
    return pl.pallas_call(
        functools.partial(_odd_body, n=n, latent=latent, slot=slot),
        out_shape=out_shape,
        grid=(nb, 2 * nt),
        in_specs=in_specs,
        out_specs=out_specs,
        scratch_shapes=scratch,
        input_output_aliases=aliases,
        compiler_params=_cparams(("arbitrary", "arbitrary")),
        name="odd_latent" if latent else "odd_context",
    )(*args)


def _rope_tables(n, rot, lane0, period):
    half = rot // 2
    nf = half // 2
    t = jnp.arange(n)
    freqs = ROPE_BASE ** (-jnp.arange(nf, dtype=F32) / nf)
    cos = jnp.ones((n, period), F32)
    sp = jnp.zeros((n, period), F32)
    sn = jnp.zeros((n, period), F32)
    for k, pos in enumerate((t // GRID_W, t % GRID_W)):
        ang = pos.astype(F32)[:, None] * freqs
        c, s_ = jnp.cos(ang), jnp.sin(ang)
        a = lane0 + k * half
        cos = cos.at[:, a:a + nf].set(c).at[:, a + nf:a + half].set(c)
        sn = sn.at[:, a:a + nf].set(-s_)
        sp = sp.at[:, a + nf:a + half].set(s_)
    reps = LANES // period
    return {"cos": jnp.tile(cos, (1, reps)), "sin_prev": jnp.tile(sp, (1, reps)),
            "sin_next": jnp.tile(sn, (1, reps))}


def _pool_counts(n):
    t = np.arange(n)
    cols = []
    for wdw in POOL_WINDOWS:
        lo = np.clip(t - wdw // 2, 0, n)
        hi = np.clip(t + wdw // 2, 0, n)
        cols.append(np.repeat((hi - lo).astype(np.float32)[:, None], POOL_GROUP, axis=1))
    return jnp.asarray(np.concatenate(cols, axis=1))


def _even_weights(w_in, g_q, g_kv, w_uq, w_ukv, w_pool, pool_scale, w_out):
    a, b_, c_ = Q_LORA, Q_LORA + KV_LORA, Q_LORA + KV_LORA + QK_ROPE
    krpad = jnp.zeros((D_MODEL, LANES), F32).at[:, 64:96].set(w_in[:, b_:c_])
    w_in_p = jnp.concatenate([w_in[:, :b_], w_in[:, c_:], krpad], axis=1)
    uq = w_uq.reshape(Q_LORA, MLA_HEADS, QK_NOPE + QK_ROPE)
    uq = jnp.pad(uq, ((0, 0), (0, 0), (0, LANES - QK_NOPE - QK_ROPE))).reshape(Q_LORA, MLA_HEADS * LANES)
    ukv = w_ukv.reshape(KV_LORA, MLA_HEADS, QK_NOPE + V_DIM)
    uk = jnp.pad(ukv[..., :QK_NOPE], ((0, 0), (0, 0), (0, LANES - QK_NOPE)))
    sel = jnp.zeros((LANES, MLA_HEADS, LANES), F32)
    sel = sel.at[64 + jnp.arange(QK_ROPE), :, 64 + jnp.arange(QK_ROPE)].set(1.0)
    w_kcat = jnp.concatenate([uk, sel], axis=0).reshape(KV_LORA + LANES, MLA_HEADS * LANES)
    w_uv = ukv[..., QK_NOPE:].reshape(KV_LORA, MLA_HEADS * V_DIM)
    w_pool_bd = jax.scipy.linalg.block_diag(*[w_pool[g] for g in range(4)])
    return {"w_in": w_in_p.astype(BF16), "g_q": g_q.reshape(1, -1), "g_kv": g_kv.reshape(1, -1),
            "w_uq": uq.astype(BF16), "w_kcat": w_kcat.astype(BF16), "w_uv": w_uv.astype(BF16),
            "w_pool": w_pool_bd.astype(BF16), "pool_scale": pool_scale.reshape(1, -1),
            "w_out": w_out.astype(BF16)}


def _ffn_layer(x, mod_l, n, gbase, per_seq, g_ffn, g_final, w_up, conv_w, conv_b, w_down, final):
    return _ffn(x, mod_l, n, gbase, per_seq, g_ffn.reshape(1, -1), w_up.astype(BF16), conv_w,
                conv_b.reshape(1, -1), w_down.astype(BF16), g_final.reshape(1, -1), final)


def kernel(x_prompt, x_sample, cache_mla_latent, cache_na_kv, cache_swa_kv, c, c_ctx, w_mod, b_mod, norm_mix, norm_ffn, norm_final, w_in_even, mla_q_norm, mla_kv_norm, w_uq, w_ukv, w_pool, pool_scale, w_out_even, w_in_odd, na_rpb, swa_sink, w_out_odd, w_up, conv_w, conv_b, w_down):
    nbp, n_p, _ = x_prompt.shape
    nbs, n_s, _ = x_sample.shape
    depth = w_mod.shape[0]
    xp = x_prompt.reshape(nbp * n_p, D_MODEL)
    xs = x_sample.reshape(nbs * n_s, D_MODEL)

    c_all = jnp.zeros((8, D_MODEL), F32).at[0].set(c_ctx).at[1:1 + nbs].set(c)
    mod = _adaln(c_all, w_mod, b_mod).reshape(depth, 8, N_MOD, D_MODEL)

    tab_mla = _rope_tables(n_s, QK_ROPE, 64, LANES)
    tab_swa = _rope_tables(n_s, HEAD_DIM, 0, HEAD_DIM)
    cnt_p, cnt_s = _pool_counts(n_p), _pool_counts(n_s)

    nh = na_rpb.shape[1]
    blocks = _na_bias_blocks(na_rpb.reshape((-1,) + na_rpb.shape[2:]))
    bias_all = jnp.stack([jnp.concatenate([blocks[:, v + i] for i in range(NA_KH)], axis=-1)
                          for v in range(NA_KH)], axis=1).reshape(-1, nh, NA_KH, GRID_W, NA_KH * GRID_W)

    n_even, n_odd = (depth + 1) // 2, depth // 2
    lat = kv = None
    for l in range(depth):
        e = l // 2
        mod_l = mod[l]
        g_mix = norm_mix[l].reshape(1, -1)
        if l % 2 == 0:
            w = _even_weights(w_in_even[e], mla_q_norm[e], mla_kv_norm[e], w_uq[e], w_ukv[e], w_pool[e],
                              pool_scale[e], w_out_even[e])
            ctx = cache_mla_latent[:, e]
            ctx = jnp.concatenate([ctx[..., :KV_LORA], jnp.zeros(ctx.shape[:2] + (64,), F32),
                                   ctx[..., KV_LORA:], jnp.zeros(ctx.shape[:2] + (32,), F32)],
                                  axis=-1).astype(BF16)
            xp, lat = _even_mixer(xp, mod_l, n_p, 0, False, g_mix, w, {"cnt": cnt_p}, None,
                                  slot=e, n_slots=n_even, prev_lat=lat)
            (xs,) = _even_mixer(xs, mod_l, n_s, 1, True, g_mix, w, dict(tab_mla, cnt=cnt_s), ctx)
        else:
            w = {"w_in": w_in_odd[e].astype(BF16), "w_out": w_out_odd[e].astype(BF16), "sink": swa_sink[e]}
            cna = cache_na_kv[:, e].reshape(nbs, PAST_LEN, 2 * NA_HEADS * HEAD_DIM).astype(BF16)
            csw = cache_swa_kv[:, e].reshape(nbs, PAST_LEN, 2 * SWA_KV_HEADS, HEAD_DIM)
            csw = jnp.concatenate([csw, csw], axis=-1).reshape(nbs, PAST_LEN, 512).astype(BF16)
            xp, *kv = _odd_mixer(xp, mod_l, n_p, 0, False, g_mix, w, None, None, None, None,
                                 slot=e, n_slots=n_odd, prev=kv)
            (xs,) = _odd_mixer(xs, mod_l, n_s, 1, True, g_mix, w, tab_swa, bias_all[e], cna, csw)
        ffn_w = (norm_ffn[l], norm_final, w_up[l], conv_w[l], conv_b[l], w_down[l], l == depth - 1)
        xp = _ffn_layer(xp, mod_l, n_p, 0, False, *ffn_w)
        xs = _ffn_layer(xs, mod_l, n_s, 1, True, *ffn_w)

    return (xp.reshape(nbp, n_p, D_MODEL), xs.reshape(nbs, n_s, D_MODEL), lat,
            kv[0].reshape(nbp, n_odd, n_p, 2, NA_HEADS, HEAD_DIM),
            kv[1].reshape(nbp, n_odd, n_p, 2, SWA_KV_HEADS, HEAD_DIM))
```

```python
import functools

import numpy as np
import jax
import jax.numpy as jnp
from jax import lax
from jax.experimental import pallas as pl
from jax.experimental.pallas import tpu as pltpu

F32 = jnp.float32
BF16 = jnp.bfloat16

D_MODEL = 1024
GRID_W = 64
N_MOD = 6
EPS = 1e-6
ROPE_BASE = 10000.0
NEG = -1e30
MLA_HEADS = 12
Q_LORA = 384
KV_LORA = 256
QK_NOPE = 64
QK_ROPE = 32
V_DIM = 64
MLA_SCALE = (QK_NOPE + QK_ROPE) ** -0.5
LOG2E = 1.4426950408889634
POOL_WINDOWS = (2, 4, 8, 16)
POOL_GROUP = 64
POOL_DIM = 256
NA_HEADS = 8
NA_KH = 8
NA_KW = 16
SWA_HEADS = 8
SWA_KV_HEADS = 2
SWA_WINDOW = 128
HEAD_DIM = 64
D_FF = 2816
PAST_LEN = 512

LANES = 128
TM = 256
TQ_MLA = 512
SWA_STACK = 2
FF_CHUNK = 256
N_FF_CHUNKS = D_FF // FF_CHUNK
POOL_PAD = 16
VMEM_LIMIT = 60000 * 1024


def _cparams(sem):
    return pltpu.CompilerParams(dimension_semantics=sem, vmem_limit_bytes=VMEM_LIMIT)


def _rms(x, g):
    return x * lax.rsqrt(jnp.mean(x * x, axis=-1, keepdims=True) + EPS) * g


def _silu(x):
    return x * (1.0 / (1.0 + jnp.exp(-x)))


def _dot(a, b):
    return jnp.dot(a, b, preferred_element_type=F32)


def _dot_t(a, b):
    return lax.dot_general(a, b, (((1,), (1,)), ((), ())), preferred_element_type=F32)


def _rope(x, cos, sin_prev, sin_next, shift):
    w = x.shape[-1]
    return x * cos + pltpu.roll(x, shift, 1) * sin_prev + pltpu.roll(x, w - shift, 1) * sin_next


def _const_spec(shape):
    nd = len(shape)
    return pl.BlockSpec(shape, lambda *_: (0,) * nd, pipeline_mode=pl.Buffered(1))


def _layer_spec(arr, l):
    shape = (1,) + arr.shape[1:]
    tail = (0,) * (arr.ndim - 1)
    return pl.BlockSpec(shape, lambda *_: (l,) + tail, pipeline_mode=pl.Buffered(1))


def _adaln_body(c_ref, w_ref, b_ref, o_ref):
    a = _silu(c_ref[...]).astype(BF16)
    o_ref[0] = _dot(a, w_ref[0].astype(BF16)) + b_ref[0]


def _adaln(c_all, w_mod, b_mod):
    depth, _, width = w_mod.shape
    tn = 1536
    return pl.pallas_call(
        _adaln_body,
        out_shape=jax.ShapeDtypeStruct((depth, 8, width), F32),
        grid=(depth, width // tn),
        in_specs=[
            pl.BlockSpec((8, D_MODEL), lambda l, j: (0, 0)),
            pl.BlockSpec((1, D_MODEL, tn), lambda l, j: (l, 0, j)),
            pl.BlockSpec((1, 1, tn), lambda l, j: (l, 0, j)),
        ],
        out_specs=pl.BlockSpec((1, 8, tn), lambda l, j: (l, 0, j)),
        compiler_params=_cparams(("arbitrary", "arbitrary")),
        name="adaln",
    )(c_all, w_mod, b_mod.reshape(depth, 1, width))


def _ffn_body(x_ref, x_prev_ref, x_next_ref, mod_ref, g_ref, wup_ref, cw_ref, cb_ref, wdn_ref, gf_ref,
              o_ref, act_ref, *, n, final):
    i = pl.program_id(0)
    m = mod_ref[0, 0]
    shift, scale, gate = m[3:4], m[4:5], m[5:6]
    g = g_ref[0]

    def hn(x):
        return _rms(x, g) * (1.0 + scale) + shift

    x = x_ref[...]
    tile_in_seq = i % (n // TM)
    keep_prev = jnp.where(tile_in_seq == 0, 0.0, 1.0)
    keep_next = jnp.where(tile_in_seq == n // TM - 1, 0.0, 1.0)
    hext = jnp.concatenate([hn(x_prev_ref[...]) * keep_prev, hn(x), hn(x_next_ref[...]) * keep_next],
                           axis=0).astype(BF16)

    def conv(c0):
        u = _dot(hext, wup_ref[0, :, c0:c0 + FF_CHUNK])
        w = cw_ref[0, :, c0:c0 + FF_CHUNK]
        return (pltpu.roll(u, 1, 0)[8:8 + TM] * w[0:1] + u[8:8 + TM] * w[1:2]
                + pltpu.roll(u, TM + 15, 0)[8:8 + TM] * w[2:3] + cb_ref[0, :, c0:c0 + FF_CHUNK])

    for j in range(N_FF_CHUNKS):
        act_ref[:, j * FF_CHUNK:(j + 1) * FF_CHUNK] = (
            _silu(conv(D_FF + j * FF_CHUNK)) * conv(j * FF_CHUNK)).astype(BF16)
    y = x + gate * _dot(act_ref[...], wdn_ref[0])
    if final:
        y = _rms(y, gf_ref[...])
    o_ref[...] = y


def _ffn(x, n, latent, mod, l, w, final):
    t = x.shape[0]
    nblk8 = t // 8

    def gmap(i):
        return (l, 1 + i // (n // TM) if latent else 0, 0, 0)

    weights = [w["g_ffn"], w["w_up"], w["conv_w"], w["conv_b"], w["w_down"]]
    return pl.pallas_call(
        functools.partial(_ffn_body, n=n, final=final),
        out_shape=jax.ShapeDtypeStruct((t, D_MODEL), F32),
        grid=(t // TM,),
        in_specs=[
            pl.BlockSpec((TM, D_MODEL), lambda i: (i, 0)),
            pl.BlockSpec((8, D_MODEL), lambda i: (jnp.maximum(i * (TM // 8) - 1, 0), 0)),
            pl.BlockSpec((8, D_MODEL), lambda i: (jnp.minimum((i + 1) * (TM // 8), nblk8 - 1), 0)),
            pl.BlockSpec((1, 1, N_MOD, D_MODEL), gmap),
            *[_layer_spec(a, l) for a in weights],
            _const_spec((1, D_MODEL)),
        ],
        out_specs=pl.BlockSpec((TM, D_MODEL), lambda i: (i, 0)),
        scratch_shapes=[pltpu.VMEM((TM, D_FF), BF16)],
        compiler_params=_cparams(("arbitrary",)),
        name="conv_ffn",
    )(x, x, x, mod, *weights, w["g_final"])


def _state_specs(nb, n_slots, n, width, slot, prev, tile_of):
    shape = jax.ShapeDtypeStruct((nb, n_slots, n, width), F32)
    if prev is None:
        return shape, pl.BlockSpec((1, n_slots, TM, width), lambda b, s: (b, 0, tile_of(s), 0))
    return shape, pl.BlockSpec((1, 1, TM, width), lambda b, s: (b, slot, tile_of(s), 0))


def _write_state(ref, slot, val):
    if ref.shape[1] == 1:
        ref[0, 0] = val
    else:
        for k in range(ref.shape[1]):
            ref[0, k] = val if k == slot else jnp.zeros_like(val)


def _even_body(*refs, n, tq, latent, slot, heads_per_step):
    if latent:
        (x_ref, xq_ref, mod_ref, g_ref, win_ref, gq_ref, gkv_ref, wuq_ref, wkc_ref, wuv_ref, wpool_ref,
         pscale_ref, cnt_ref, wout_ref, cos_ref, sp_ref, sn_ref, ctx_ref,
         o_ref, qs, ks, vs, xps, osc) = refs
        lat_ref = None
    else:
        (x_ref, xq_ref, mod_ref, g_ref, win_ref, gq_ref, gkv_ref, wuq_ref, wkc_ref, wuv_ref, wpool_ref,
         pscale_ref, cnt_ref, wout_ref) = refs[:14]
        o_ref, lat_ref, qs, ks, vs, xps, osc = refs[-7:]
    nt = n // TM
    s = pl.program_id(1)
    m = mod_ref[0, 0]
    lane = lax.broadcasted_iota(jnp.int32, (TM, LANES), 1)
    lo = lax.broadcasted_iota(jnp.int32, (tq, LANES), 1) < V_DIM

    def store_v(v, rows):
        for p in range(MLA_HEADS // 2):
            vp = v[:, LANES * p:LANES * (p + 1)]
            ln = lane[:v.shape[0]]
            vs[2 * p, rows, :] = jnp.where(ln < V_DIM, vp, jnp.where(ln == V_DIM, 1.0, 0.0)).astype(BF16)
            vs[2 * p + 1, rows, :] = jnp.where(ln >= V_DIM, vp, jnp.where(ln == 0, 1.0, 0.0)).astype(BF16)

    @pl.when(s < nt)
    def _project():
        r0 = pl.multiple_of(s * TM, TM)
        h = (_rms(x_ref[...], g_ref[0]) * (1.0 + m[1:2]) + m[0:1]).astype(BF16)
        z = _dot(h, win_ref[0])
        qn = _rms(z[:, :Q_LORA], gq_ref[0]).astype(BF16)
        q = _dot(qn, wuq_ref[0])
        latc = _rms(z[:, Q_LORA:Q_LORA + KV_LORA], gkv_ref[0])
        krp = z[:, 896:1024]
        if latent:
            cos, sp, sn = cos_ref[...], sp_ref[...], sn_ref[...]
            krp = _rope(krp, cos, sp, sn, 8)
        if lat_ref is not None:
            kr0 = pltpu.roll(krp, LANES - 64, 1)
            _write_state(lat_ref, slot, jnp.concatenate([latc, kr0[:, :QK_ROPE]], axis=1))
        lb = jnp.concatenate([latc, krp], axis=1).astype(BF16)
        kc = _dot(lb, wkc_ref[0])
        v = _dot(lb[:, :KV_LORA], wuv_ref[0])
        for hd in range(MLA_HEADS):
            qh = q[:, LANES * hd:LANES * (hd + 1)]
            if latent:
                qh = _rope(qh, cos, sp, sn, 8)
            qs[hd, pl.ds(r0, TM), :] = (qh * (MLA_SCALE * LOG2E)).astype(BF16)
            ks[hd, pl.ds(r0, TM), :] = kc[:, LANES * hd:LANES * (hd + 1)].astype(BF16)
        store_v(v, pl.ds(r0, TM))
        xps[pl.ds(POOL_PAD + r0, TM), :] = z[:, 640:896]

        @pl.when(s == 0)
        def _first():
            xps[0:POOL_PAD, :] = jnp.zeros((POOL_PAD, POOL_DIM), F32)
            xps[n + POOL_PAD:n + 2 * POOL_PAD, :] = jnp.zeros((POOL_PAD, POOL_DIM), F32)
            if latent:
                cb = ctx_ref[0, 0]
                kcc = _dot(cb, wkc_ref[0])
                for hd in range(MLA_HEADS):
                    ks[hd, n:n + PAST_LEN, :] = kcc[:, LANES * hd:LANES * (hd + 1)].astype(BF16)
                for half in range(PAST_LEN // TM):
                    vc = _dot(cb[TM * half:TM * (half + 1), :KV_LORA], wuv_ref[0])
                    store_v(vc, pl.ds(n + TM * half, TM))

    @pl.when(s >= nt)
    def _attend():
        q0 = pl.multiple_of((s - nt) * tq, tq)

        def head(hd, l_lane):
            sc = _dot_t(qs[hd, pl.ds(q0, tq), :], ks[hd])
            e = jnp.exp2(sc - jnp.max(sc, axis=-1, keepdims=True))
            o = _dot(e.astype(BF16), vs[hd])
            return o * (1.0 / o[:, l_lane:l_lane + 1])

        def pair(p):
            osc[p] = jnp.where(lo, head(2 * p, V_DIM), head(2 * p + 1, 0)).astype(BF16)

        pairs_per_step = heads_per_step // 2
        if pairs_per_step == MLA_HEADS // 2:
            for p in range(MLA_HEADS // 2):
                pair(p)
        else:
            def step(i, carry):
                for j in range(pairs_per_step):
                    pair(i * pairs_per_step + j)
                return carry
            lax.fori_loop(0, MLA_HEADS // 2 // pairs_per_step, step, 0)

        rows = tq + 2 * POOL_PAD
        a0 = xps[pl.ds(q0, rows), :]
        s1 = a0 + pltpu.roll(a0, 1, 0)
        s2 = s1 + pltpu.roll(s1, 2, 0)
        s4 = s2 + pltpu.roll(s2, 4, 0)
        s8 = s4 + pltpu.roll(s4, 8, 0)
        pl_lane = lax.broadcasted_iota(jnp.int32, (rows, POOL_DIM), 1)
        win = jnp.where(pl_lane < 64, s1,
                        jnp.where(pl_lane < 128, pltpu.roll(s2, rows - 1, 0),
                                  jnp.where(pl_lane < 192, pltpu.roll(s4, rows - 3, 0),
                                            pltpu.roll(s8, rows - 7, 0))))
        pooled = win[POOL_PAD:POOL_PAD + tq] / cnt_ref[...] - a0[POOL_PAD:POOL_PAD + tq]
        ypool = _dot(pooled.astype(BF16), wpool_ref[0]) * pscale_ref[0]

        mix = jnp.concatenate([osc[p] for p in range(MLA_HEADS // 2)] + [ypool.astype(BF16)], axis=1)
        o_ref[...] = xq_ref[...] + m[2:3] * _dot(mix, wout_ref[0])


def _even_mixer(x, mod, l, n, latent, g_mix, w, tables, ctx, n_slots=1, prev_lat=None):
    slot = l // 2
    t = x.shape[0]
    nb = t // n
    nt = n // TM
    tq = min(n, TQ_MLA)
    ntq = n // tq
    nk = n + (PAST_LEN if latent else 0)

    def xmap(b, s):
        return (b * nt + jnp.minimum(s, nt - 1), 0)

    def qmap(b, s):
        return (b * ntq + jnp.maximum(s - nt, 0), 0)

    def gmap(b, s):
        return (l, 1 + b if latent else 0, 0, 0)

    names = ["w_in", "g_q", "g_kv", "w_uq", "w_kcat", "w_uv", "w_pool", "pool_scale"]
    in_specs = [
        pl.BlockSpec((TM, D_MODEL), xmap),
        pl.BlockSpec((tq, D_MODEL), qmap),
        pl.BlockSpec((1, 1, N_MOD, D_MODEL), gmap),
        _layer_spec(g_mix, l),
        *[_layer_spec(w[k], slot) for k in names],
        pl.BlockSpec((tq, POOL_DIM), lambda b, s: (jnp.maximum(s - nt, 0), 0)),
        _layer_spec(w["w_out"], slot),
    ]
    args = [x, x, mod, g_mix, *[w[k] for k in names], tables["cnt"], w["w_out"]]
    out_shape = [jax.ShapeDtypeStruct((t, D_MODEL), F32)]
    out_specs = [pl.BlockSpec((tq, D_MODEL), qmap)]
    aliases = {}
    if latent:
        tspec = pl.BlockSpec((TM, LANES), lambda b, s: (jnp.minimum(s, nt - 1), 0))
        in_specs += [tspec, tspec, tspec,
                     pl.BlockSpec((1, 1, PAST_LEN, 384), lambda b, s: (b, slot, 0, 0))]
        args += [tables["cos"], tables["sin_prev"], tables["sin_next"], ctx]
    else:
        shape, spec = _state_specs(nb, n_slots, n, KV_LORA + QK_ROPE, slot, prev_lat,
                                   lambda s: jnp.minimum(s, nt - 1))
        out_shape.append(shape)
        out_specs.append(spec)
        if prev_lat is not None:
            aliases = {len(args): 1}
            in_specs.append(pl.BlockSpec(memory_space=pl.ANY))
            args.append(prev_lat)
    scratch = [
        pltpu.VMEM((MLA_HEADS, n, LANES), BF16),
        pltpu.VMEM((MLA_HEADS, nk, LANES), BF16),
        pltpu.VMEM((MLA_HEADS, nk, LANES), BF16),
        pltpu.VMEM((n + 2 * POOL_PAD, POOL_DIM), F32),
        pltpu.VMEM((MLA_HEADS // 2, tq, LANES), BF16),
    ]
    return pl.pallas_call(
        functools.partial(_even_body, n=n, tq=tq, latent=latent, slot=slot,
                          heads_per_step=MLA_HEADS),
        out_shape=out_shape,
        grid=(nb, nt + ntq),
        in_specs=in_specs,
        out_specs=out_specs,
        scratch_shapes=scratch,
        input_output_aliases=aliases,
        compiler_params=_cparams(("arbitrary", "arbitrary")),
        name="even_latent" if latent else "even_context",
    )(*args)


def _bias_body(rpb_ref, o_ref):
    qc = lax.broadcasted_iota(jnp.int32, (GRID_W, LANES), 0)
    kc = lax.broadcasted_iota(jnp.int32, (GRID_W, LANES), 1)
    c0 = jnp.clip(qc - NA_KW // 2, 0, GRID_W - NA_KW)
    inwin = (kc >= c0) & (kc < c0 + NA_KW)
    blocks = []
    for dr in range(2 * NA_KH - 1):
        row = jnp.broadcast_to(rpb_ref[0, dr:dr + 1, :], (GRID_W, LANES))
        toep = pltpu.roll(row, LANES - (NA_KW - 1), 1, stride=1, stride_axis=0)
        blocks.append(jnp.where(inwin, toep, NEG)[:, :GRID_W])
    for v in range(NA_KH):
        for i in range(NA_KH):
            o_ref[0, v, :, GRID_W * i:GRID_W * (i + 1)] = blocks[v + i]


def _na_bias(rpb):
    nh, ndr, nj = rpb.shape
    rpb = jnp.pad(rpb, ((0, 0), (0, 0), (0, LANES - nj)))
    return pl.pallas_call(
        _bias_body,
        out_shape=jax.ShapeDtypeStruct((nh, NA_KH, GRID_W, NA_KH * GRID_W), F32),
        grid=(nh,),
        in_specs=[pl.BlockSpec((1, ndr, LANES), lambda h: (h, 0, 0))],
        out_specs=pl.BlockSpec((1, NA_KH, GRID_W, NA_KH * GRID_W), lambda h: (h, 0, 0, 0)),
        compiler_params=_cparams(("arbitrary",)),
        name="na_bias",
    )(rpb)


def _softmax_pv(parts, extra=None):
    mx = functools.reduce(jnp.maximum, [jnp.max(sc, axis=-1, keepdims=True) for sc, _ in parts])
    if extra is not None:
        mx = jnp.maximum(mx, extra)
    l = jnp.exp(extra - mx) if extra is not None else 0.0
    acc = None
    for sc, val in parts:
        e = jnp.exp(sc - mx)
        l = l + jnp.sum(e, axis=-1, keepdims=True)
        o = val(e.astype(BF16)) if callable(val) else _dot(e.astype(BF16), val)
        acc = o if acc is None else acc + o
    return acc / l


def _odd_body(*refs, n, latent, slot):
    if latent:
        (x_ref, mod_ref, g_ref, win_ref, sink_ref, wout_ref, cos_ref, sp_ref, sn_ref, bias_ref,
         cna_ref, csw_ref,
         o_ref, qna, kna, vna, qsw, ksw, vsw, osc) = refs
        nakv_ref = swkv_ref = None
    else:
        x_ref, mod_ref, g_ref, win_ref, sink_ref, wout_ref = refs[:6]
        o_ref, nakv_ref, swkv_ref, qna, kna, vna, qsw, ksw, vsw, osc = refs[-10:]
    nt = n // TM
    s = pl.program_id(1)
    m = mod_ref[0, 0]
    sw_off = SWA_WINDOW if latent else 0
    lane = lax.broadcasted_iota(jnp.int32, (TM, LANES), 1)
    lo = lane < HEAD_DIM

    @pl.when(s < nt)
    def _project():
        r0 = pl.multiple_of(s * TM, TM)
        h = (_rms(x_ref[...], g_ref[0]) * (1.0 + m[1:2]) + m[0:1]).astype(BF16)
        z = _dot(h, win_ref[0])
        if nakv_ref is not None:
            _write_state(nakv_ref, slot, z[:, 512:1536])
            _write_state(swkv_ref, slot, z[:, 2048:2304])
        scale = HEAD_DIM ** -0.5
        for p in range(NA_HEADS // 2):
            qp = z[:, LANES * p:LANES * (p + 1)] * scale
            qna[2 * p, pl.ds(r0, TM), :] = jnp.where(lo, qp, 0.0).astype(BF16)
            qna[2 * p + 1, pl.ds(r0, TM), :] = jnp.where(lo, 0.0, qp).astype(BF16)
            kna[p, pl.ds(r0, TM), :] = z[:, 512 + LANES * p:512 + LANES * (p + 1)].astype(BF16)
            vna[p, pl.ds(r0, TM), :] = z[:, 1024 + LANES * p:1024 + LANES * (p + 1)].astype(BF16)
        if latent:
            cos, sp, sn = cos_ref[...], sp_ref[...], sn_ref[...]
        for p in range(SWA_HEADS // 2):
            qp = z[:, 1536 + LANES * p:1536 + LANES * (p + 1)]
            if latent:
                qp = _rope(qp, cos, sp, sn, 16)
            qp = qp * scale
            qsw[2 * p, pl.ds(r0, TM), :] = jnp.where(lo, qp, 0.0).astype(BF16)
            qsw[2 * p + 1, pl.ds(r0, TM), :] = jnp.where(lo, 0.0, qp).astype(BF16)
        k = z[:, 2048:2176]
        if latent:
            k = _rope(k, cos, sp, sn, 16)
        v = z[:, 2176:2304]
        ksw_, vsw_ = pltpu.roll(k, HEAD_DIM, 1), pltpu.roll(v, HEAD_DIM, 1)
        ksw[0, pl.ds(sw_off + r0, TM), :] = jnp.where(lo, k, ksw_).astype(BF16)
        ksw[1, pl.ds(sw_off + r0, TM), :] = jnp.where(lo, ksw_, k).astype(BF16)
        vsw[0, pl.ds(sw_off + r0, TM), :] = jnp.where(lo, v, vsw_).astype(BF16)
        vsw[1, pl.ds(sw_off + r0, TM), :] = jnp.where(lo, vsw_, v).astype(BF16)

        if latent:
            @pl.when(s == 0)
            def _first():
                cna = cna_ref[0, 0]
                for p in range(NA_HEADS // 2):
                    kna[p, n:n + PAST_LEN, :] = cna[:, LANES * p:LANES * (p + 1)]
                    vna[p, n:n + PAST_LEN, :] = cna[:, 512 + LANES * p:512 + LANES * (p + 1)]
                csw = csw_ref[0, 0]
                zpad = jnp.zeros((SWA_WINDOW, LANES), BF16)
                for kv in range(SWA_KV_HEADS):
                    ksw[kv, 0:SWA_WINDOW, :] = zpad
                    vsw[kv, 0:SWA_WINDOW, :] = zpad
                    ksw[kv, n + SWA_WINDOW:n + 2 * SWA_WINDOW, :] = zpad
                    vsw[kv, n + SWA_WINDOW:n + 2 * SWA_WINDOW, :] = zpad
                    ksw[kv, n + 2 * SWA_WINDOW:n + 2 * SWA_WINDOW + PAST_LEN, :] = csw[:, LANES * kv:LANES * (kv + 1)]
                    vsw[kv, n + 2 * SWA_WINDOW:n + 2 * SWA_WINDOW + PAST_LEN, :] = csw[:, 256 + LANES * kv:256 + LANES * (kv + 1)]

    @pl.when(s >= nt)
    def _attend():
        ti = s - nt
        q0 = pl.multiple_of(ti * TM, TM)
        rows_per_tile = TM // GRID_W
        nrows = n // GRID_W
        nloc = NA_KH * GRID_W

        def na_pair(p):
            outs = []
            for a in (0, 1):
                hd = 2 * p + a
                qh = qna[hd, pl.ds(q0, TM), :]
                if latent:
                    s_loc, starts = [], []
                    for r in range(rows_per_tile):
                        row = ti * rows_per_tile + r
                        rs = jnp.clip(row - NA_KH // 2, 0, nrows - NA_KH)
                        k0 = pl.multiple_of(rs * GRID_W, GRID_W)
                        starts.append(k0)
                        s_loc.append(_dot_t(qh[GRID_W * r:GRID_W * (r + 1)], kna[p, pl.ds(k0, nloc), :])
                                     + bias_ref[0, hd, rs - row + NA_KH - 1])
                    s_loc = jnp.concatenate(s_loc, axis=0)
                    s_ctx = _dot_t(qh, kna[p, pl.ds(n, PAST_LEN), :])

                    def pv_loc(e):
                        return jnp.concatenate(
                            [_dot(e[GRID_W * r:GRID_W * (r + 1)], vna[p, pl.ds(starts[r], nloc), :])
                             for r in range(rows_per_tile)], axis=0)

                    outs.append(_softmax_pv([(s_loc, pv_loc), (s_ctx, vna[p, pl.ds(n, PAST_LEN), :])]))
                else:
                    outs.append(_softmax_pv([(_dot_t(qh, kna[p]), vna[p])]))
            osc[p] = jnp.where(lo, outs[0], outs[1]).astype(BF16)

        if latent:
            def na_step(p, carry):
                na_pair(p)
                return carry
            lax.fori_loop(0, NA_HEADS // 2, na_step, 0)
        else:
            for p in range(NA_HEADS // 2):
                na_pair(p)

        group = SWA_HEADS // SWA_KV_HEADS
        stack = SWA_STACK
        for c in range(SWA_HEADS // stack):
            kv = c * stack // group
            qst = jnp.concatenate([qsw[stack * c + g, pl.ds(q0, TM), :] for g in range(stack)], axis=0)
            sink = jnp.concatenate(
                [jnp.full((TM, 1), sink_ref[slot, stack * c + g], F32) for g in range(stack)], axis=0)
            if latent:
                span = TM + 2 * SWA_WINDOW
                s_loc = _dot_t(qst, ksw[kv, pl.ds(q0, span), :])
                a_i = lax.broadcasted_iota(jnp.int32, (TM, span), 0)
                b_i = lax.broadcasted_iota(jnp.int32, (TM, span), 1)
                key = q0 - SWA_WINDOW + b_i
                ok = (b_i >= a_i) & (b_i <= a_i + 2 * SWA_WINDOW) & (key >= 0) & (key < n)
                s_loc = jnp.concatenate(
                    [jnp.where(ok, s_loc[TM * g:TM * (g + 1)], NEG) for g in range(stack)], axis=0)
                c0 = n + 2 * SWA_WINDOW
                s_ctx = _dot_t(qst, ksw[kv, pl.ds(c0, PAST_LEN), :])
                o = _softmax_pv([(s_loc, vsw[kv, pl.ds(q0, span), :]),
                                 (s_ctx, vsw[kv, pl.ds(c0, PAST_LEN), :])], extra=sink)
            else:
                o = _softmax_pv([(_dot_t(qst, ksw[kv]), vsw[kv])], extra=sink)
            for j in range(stack // 2):
                osc[NA_HEADS // 2 + (stack // 2) * c + j] = jnp.where(
                    lo, o[TM * 2 * j:TM * (2 * j + 1)], o[TM * (2 * j + 1):TM * (2 * j + 2)]).astype(BF16)

        mix = jnp.concatenate([osc[p] for p in range(8)], axis=1)
        o_ref[...] = x_ref[...] + m[2:3] * _dot(mix, wout_ref[0])


def _odd_mixer(x, mod, l, n, latent, g_mix, w, tables, bias, cna, csw, n_slots=1, prev=None):
    slot = l // 2
    t = x.shape[0]
    nb = t // n
    nt = n // TM
    nk = n + (PAST_LEN if latent else 0)
    nsw = n + (2 * SWA_WINDOW + PAST_LEN if latent else 0)

    def xmap(b, s):
        return (b * nt + s % nt, 0)

    def gmap(b, s):
        return (l, 1 + b if latent else 0, 0, 0)

    in_specs = [
        pl.BlockSpec((TM, D_MODEL), xmap),
        pl.BlockSpec((1, 1, N_MOD, D_MODEL), gmap),
        _layer_spec(g_mix, l),
        _layer_spec(w["w_in"], slot),
        pl.BlockSpec(memory_space=pltpu.SMEM),
        _layer_spec(w["w_out"], slot),
    ]
    args = [x, mod, g_mix, w["w_in"], w["sink"], w["w_out"]]
    out_shape = [jax.ShapeDtypeStruct((t, D_MODEL), F32)]
    out_specs = [pl.BlockSpec((TM, D_MODEL), lambda b, s: (b * nt + jnp.maximum(s - nt, 0), 0))]
    aliases = {}
    if latent:
        tspec = pl.BlockSpec((TM, LANES), lambda b, s: (jnp.minimum(s, nt - 1), 0))
        in_specs += [tspec, tspec, tspec, _layer_spec(bias, slot),
                     pl.BlockSpec((1, 1, PAST_LEN, 1024), lambda b, s: (b, slot, 0, 0)),
                     pl.BlockSpec((1, 1, PAST_LEN, 512), lambda b, s: (b, slot, 0, 0))]
        args += [tables["cos"], tables["sin_prev"], tables["sin_next"], bias, cna, csw]
    else:
        for k, width in enumerate((2 * NA_HEADS * HEAD_DIM, 2 * SWA_KV_HEADS * HEAD_DIM)):
            shape, spec = _state_specs(nb, n_slots, n, width, slot, prev, lambda s: jnp.minimum(s, nt - 1))
            out_shape.append(shape)
            out_specs.append(spec)
            if prev is not None:
                aliases[len(args)] = 1 + k
                in_specs.append(pl.BlockSpec(memory_space=pl.ANY))
                args.append(prev[k])
    scratch = [
        pltpu.VMEM((NA_HEADS, n, LANES), BF16),
        pltpu.VMEM((NA_HEADS // 2, nk, LANES), BF16),
        pltpu.VMEM((NA_HEADS // 2, nk, LANES), BF16),
        pltpu.VMEM((SWA_HEADS, n, LANES), BF16),
        pltpu.VMEM((SWA_KV_HEADS, nsw, LANES), BF16),
        pltpu.VMEM((SWA_KV_HEADS, nsw, LANES), BF16),
        pltpu.VMEM((8, TM, LANES), BF16),
    ]
    return pl.pallas_call(
        functools.partial(_odd_body, n=n, latent=latent, slot=slot),
        out_shape=out_shape,
        grid=(nb, 2 * nt),
        in_specs=in_specs,
        out_specs=out_specs,
        scratch_shapes=scratch,
        input_output_aliases=aliases,
        compiler_params=_cparams(("arbitrary", "arbitrary")),
        name="odd_latent" if latent else "odd_context",
    )(*args)


def _rope_tables(n, rot, lane0, period):
    half = rot // 2
    nf = half // 2
    t = np.arange(n)
    freqs = (ROPE_BASE ** (-np.arange(nf, dtype=np.float32) / nf)).astype(np.float32)
    cos = np.ones((n, period), np.float32)
    sp = np.zeros((n, period), np.float32)
    sn = np.zeros((n, period), np.float32)
    for k, pos in enumerate((t // GRID_W, t % GRID_W)):
        ang = pos.astype(np.float32)[:, None] * freqs
        c, s_ = np.cos(ang), np.sin(ang)
        a = lane0 + k * half
        cos[:, a:a + nf] = c
        cos[:, a + nf:a + half] = c
        sn[:, a:a + nf] = -s_
        sp[:, a + nf:a + half] = s_
    reps = LANES // period
    return {"cos": jnp.asarray(np.tile(cos, (1, reps))), "sin_prev": jnp.asarray(np.tile(sp, (1, reps))),
            "sin_next": jnp.asarray(np.tile(sn, (1, reps)))}


def _pool_counts(n):
    t = np.arange(n)
    cols = []
    for wdw in POOL_WINDOWS:
        lo = np.clip(t - wdw // 2, 0, n)
        hi = np.clip(t + wdw // 2, 0, n)
        cols.append(np.repeat((hi - lo).astype(np.float32)[:, None], POOL_GROUP, axis=1))
    return jnp.asarray(np.concatenate(cols, axis=1))


def _even_weights(w_in, g_q, g_kv, w_uq, w_ukv, w_pool, pool_scale, w_out):
    ne = w_in.shape[0]
    b_, c_ = Q_LORA + KV_LORA, Q_LORA + KV_LORA + QK_ROPE
    krpad = jnp.pad(w_in[..., b_:c_], ((0, 0), (0, 0), (64, LANES - 64 - QK_ROPE)))
    w_in_p = jnp.concatenate([w_in[..., :b_], w_in[..., c_:], krpad], axis=-1)
    uq = w_uq.reshape(ne, Q_LORA, MLA_HEADS, QK_NOPE + QK_ROPE)
    uq = jnp.pad(uq, ((0, 0),) * 3 + ((0, LANES - QK_NOPE - QK_ROPE),)).reshape(ne, Q_LORA, MLA_HEADS * LANES)
    ukv = w_ukv.reshape(ne, KV_LORA, MLA_HEADS, QK_NOPE + V_DIM)
    uk = jnp.pad(ukv[..., :QK_NOPE], ((0, 0),) * 3 + ((0, LANES - QK_NOPE),))
    sel = np.zeros((LANES, MLA_HEADS, LANES), np.float32)
    sel[64 + np.arange(QK_ROPE), :, 64 + np.arange(QK_ROPE)] = 1.0
    sel = jnp.broadcast_to(jnp.asarray(sel), (ne,) + sel.shape)
    w_kcat = jnp.concatenate([uk, sel], axis=1).reshape(ne, KV_LORA + LANES, MLA_HEADS * LANES)
    w_uv = ukv[..., QK_NOPE:].reshape(ne, KV_LORA, MLA_HEADS * V_DIM)
    eye = jnp.asarray(np.eye(len(POOL_WINDOWS), dtype=np.float32))
    w_pool_bd = (w_pool[:, :, :, None, :] * eye[None, :, None, :, None]).reshape(ne, POOL_DIM, POOL_DIM)
    return {"w_in": w_in_p.astype(BF16), "g_q": g_q.reshape(ne, 1, -1), "g_kv": g_kv.reshape(ne, 1, -1),
            "w_uq": uq.astype(BF16), "w_kcat": w_kcat.astype(BF16), "w_uv": w_uv.astype(BF16),
            "w_pool": w_pool_bd.astype(BF16), "pool_scale": pool_scale.reshape(ne, 1, -1),
            "w_out": w_out.astype(BF16)}


def kernel(x_prompt, x_sample, cache_mla_latent, cache_na_kv, cache_swa_kv, c, c_ctx, w_mod, b_mod, norm_mix, norm_ffn, norm_final, w_in_even, mla_q_norm, mla_kv_norm, w_uq, w_ukv, w_pool, pool_scale, w_out_even, w_in_odd, na_rpb, swa_sink, w_out_odd, w_up, conv_w, conv_b, w_down):
    nbp, n_p, _ = x_prompt.shape
    nbs, n_s, _ = x_sample.shape
    depth = w_mod.shape[0]
    xp = x_prompt.reshape(nbp * n_p, D_MODEL)
    xs = x_sample.reshape(nbs * n_s, D_MODEL)

    c_all = jnp.zeros((8, D_MODEL), F32).at[0].set(c_ctx).at[1:1 + nbs].set(c)
    mod = _adaln(c_all, w_mod, b_mod).reshape(depth, 8, N_MOD, D_MODEL)

    tab_mla = _rope_tables(n_s, QK_ROPE, 64, LANES)
    tab_swa = _rope_tables(n_s, HEAD_DIM, 0, HEAD_DIM)
    cnt_p, cnt_s = _pool_counts(n_p), _pool_counts(n_s)

    bias_all = _na_bias(na_rpb.reshape((-1,) + na_rpb.shape[2:])).reshape(
        na_rpb.shape[:2] + (NA_KH, GRID_W, NA_KH * GRID_W))

    n_even, n_odd = (depth + 1) // 2, depth // 2
    g_mix = norm_mix.reshape(depth, 1, D_MODEL)
    w_even = _even_weights(w_in_even, mla_q_norm, mla_kv_norm, w_uq, w_ukv, w_pool, pool_scale, w_out_even)
    w_odd = {"w_in": w_in_odd.astype(BF16), "w_out": w_out_odd.astype(BF16), "sink": swa_sink}
    w_ffn = {"g_ffn": norm_ffn.reshape(depth, 1, D_MODEL), "w_up": w_up.astype(BF16), "conv_w": conv_w,
             "conv_b": conv_b.reshape(depth, 1, -1), "w_down": w_down.astype(BF16),
             "g_final": norm_final.reshape(1, -1)}
    zeros = functools.partial(jnp.zeros, dtype=F32)
    lead = cache_mla_latent.shape[:3]
    ctx = jnp.concatenate([cache_mla_latent[..., :KV_LORA], zeros(lead + (64,)),
                           cache_mla_latent[..., KV_LORA:], zeros(lead + (32,))], axis=-1).astype(BF16)
    cna = cache_na_kv.reshape(nbs, -1, PAST_LEN, 2 * NA_HEADS * HEAD_DIM).astype(BF16)
    csw = cache_swa_kv.reshape(nbs, -1, PAST_LEN, 2 * SWA_KV_HEADS, HEAD_DIM)
    csw = jnp.concatenate([csw, csw], axis=-1).reshape(nbs, -1, PAST_LEN, 512).astype(BF16)

    lat = kv = None
    for l in range(depth):
        if l % 2 == 0:
            xp, lat = _even_mixer(xp, mod, l, n_p, False, g_mix, w_even, {"cnt": cnt_p}, None,
                                  n_slots=n_even, prev_lat=lat)
            (xs,) = _even_mixer(xs, mod, l, n_s, True, g_mix, w_even, dict(tab_mla, cnt=cnt_s), ctx)
        else:
            xp, *kv = _odd_mixer(xp, mod, l, n_p, False, g_mix, w_odd, None, None, None, None,
                                 n_slots=n_odd, prev=kv)
            (xs,) = _odd_mixer(xs, mod, l, n_s, True, g_mix, w_odd, tab_swa, bias_all, cna, csw)
        xp = _ffn(xp, n_p, False, mod, l, w_ffn, l == depth - 1)
        xs = _ffn(xs, n_s, True, mod, l, w_ffn, l == depth - 1)

    return (xp.reshape(nbp, n_p, D_MODEL), xs.reshape(nbs, n_s, D_MODEL), lat,
            kv[0].reshape(nbp, n_odd, n_p, 2, NA_HEADS, HEAD_DIM),
            kv[1].reshape(nbp, n_odd, n_p, 2, SWA_KV_HEADS, HEAD_DIM))
```

```python
import functools

import numpy as np
import jax
import jax.numpy as jnp
from jax import lax
from jax.experimental import pallas as pl
from jax.experimental.pallas import tpu as pltpu

F32 = jnp.float32
BF16 = jnp.bfloat16

D_MODEL = 1024
GRID_W = 64
N_MOD = 6
EPS = 1e-6
ROPE_BASE = 10000.0
NEG = -1e30
MLA_HEADS = 12
Q_LORA = 384
KV_LORA = 256
QK_NOPE = 64
QK_ROPE = 32
V_DIM = 64
MLA_SCALE = (QK_NOPE + QK_ROPE) ** -0.5
LOG2E = 1.4426950408889634
POOL_WINDOWS = (2, 4, 8, 16)
POOL_GROUP = 64
POOL_DIM = 256
NA_HEADS = 8
NA_KH = 8
NA_KW = 16
SWA_HEADS = 8
SWA_KV_HEADS = 2
SWA_WINDOW = 128
HEAD_DIM = 64
D_FF = 2816
PAST_LEN = 512

LANES = 128
TM = 256
TQ_MLA = 512
SWA_STACK = 2
CTX_SEQS_PER_STEP = 1
FF_CHUNK = 256
N_FF_CHUNKS = D_FF // FF_CHUNK
POOL_PAD = 16
VMEM_LIMIT = 60000 * 1024


def _cparams(sem):
    return pltpu.CompilerParams(dimension_semantics=sem, vmem_limit_bytes=VMEM_LIMIT)


def _rms(x, g):
    return x * lax.rsqrt(jnp.mean(x * x, axis=-1, keepdims=True) + EPS) * g


def _silu(x):
    return x * (1.0 / (1.0 + jnp.exp(-x)))


def _dot(a, b):
    return jnp.dot(a, b, preferred_element_type=F32)


def _dot_t(a, b):
    return lax.dot_general(a, b, (((1,), (1,)), ((), ())), preferred_element_type=F32)


def _rope(x, cos, sin_prev, sin_next, shift):
    w = x.shape[-1]
    return x * cos + pltpu.roll(x, shift, 1) * sin_prev + pltpu.roll(x, w - shift, 1) * sin_next


def _const_spec(shape):
    nd = len(shape)
    return pl.BlockSpec(shape, lambda *_: (0,) * nd, pipeline_mode=pl.Buffered(1))


def _layer_spec(arr, l):
    shape = (1,) + arr.shape[1:]
    tail = (0,) * (arr.ndim - 1)
    return pl.BlockSpec(shape, lambda *_: (l,) + tail, pipeline_mode=pl.Buffered(1))


def _adaln_body(c_ref, w_ref, b_ref, o_ref):
    a = _silu(c_ref[...]).astype(BF16)
    o_ref[0] = _dot(a, w_ref[0].astype(BF16)) + b_ref[0]


def _adaln(c_all, w_mod, b_mod):
    depth, _, width = w_mod.shape
    tn = 1536
    return pl.pallas_call(
        _adaln_body,
        out_shape=jax.ShapeDtypeStruct((depth, 8, width), F32),
        grid=(depth, width // tn),
        in_specs=[
            pl.BlockSpec((8, D_MODEL), lambda l, j: (0, 0)),
            pl.BlockSpec((1, D_MODEL, tn), lambda l, j: (l, 0, j)),
            pl.BlockSpec((1, 1, tn), lambda l, j: (l, 0, j)),
        ],
        out_specs=pl.BlockSpec((1, 8, tn), lambda l, j: (l, 0, j)),
        compiler_params=_cparams(("arbitrary", "arbitrary")),
        name="adaln",
    )(c_all, w_mod, b_mod.reshape(depth, 1, width))


def _ffn_body(x_ref, x_prev_ref, x_next_ref, mod_ref, g_ref, wup_ref, cw_ref, cb_ref, wdn_ref, gf_ref,
              o_ref, act_ref, *, n, final):
    i = pl.program_id(0)
    m = mod_ref[0, 0]
    shift, scale, gate = m[3:4], m[4:5], m[5:6]
    g = g_ref[0]

    def hn(x):
        return _rms(x, g) * (1.0 + scale) + shift

    x = x_ref[...]
    tile_in_seq = i % (n // TM)
    keep_prev = jnp.where(tile_in_seq == 0, 0.0, 1.0)
    keep_next = jnp.where(tile_in_seq == n // TM - 1, 0.0, 1.0)
    hext = jnp.concatenate([hn(x_prev_ref[...]) * keep_prev, hn(x), hn(x_next_ref[...]) * keep_next],
                           axis=0).astype(BF16)

    def conv(c0):
        u = _dot(hext, wup_ref[0, :, c0:c0 + FF_CHUNK])
        w = cw_ref[0, :, c0:c0 + FF_CHUNK]
        return (pltpu.roll(u, 1, 0)[8:8 + TM] * w[0:1] + u[8:8 + TM] * w[1:2]
                + pltpu.roll(u, TM + 15, 0)[8:8 + TM] * w[2:3] + cb_ref[0, :, c0:c0 + FF_CHUNK])

    for j in range(N_FF_CHUNKS):
        act_ref[:, j * FF_CHUNK:(j + 1) * FF_CHUNK] = (
            _silu(conv(D_FF + j * FF_CHUNK)) * conv(j * FF_CHUNK)).astype(BF16)
    y = x + gate * _dot(act_ref[...], wdn_ref[0])
    if final:
        y = _rms(y, gf_ref[...])
    o_ref[...] = y


def _ffn(x, n, latent, mod, l, w, final):
    t = x.shape[0]
    nblk8 = t // 8

    def gmap(i):
        return (l, 1 + i // (n // TM) if latent else 0, 0, 0)

    weights = [w["g_ffn"], w["w_up"], w["conv_w"], w["conv_b"], w["w_down"]]
    return pl.pallas_call(
        functools.partial(_ffn_body, n=n, final=final),
        out_shape=jax.ShapeDtypeStruct((t, D_MODEL), F32),
        grid=(t // TM,),
        in_specs=[
            pl.BlockSpec((TM, D_MODEL), lambda i: (i, 0)),
            pl.BlockSpec((8, D_MODEL), lambda i: (jnp.maximum(i * (TM // 8) - 1, 0), 0)),
            pl.BlockSpec((8, D_MODEL), lambda i: (jnp.minimum((i + 1) * (TM // 8), nblk8 - 1), 0)),
            pl.BlockSpec((1, 1, N_MOD, D_MODEL), gmap),
            *[_layer_spec(a, l) for a in weights],
            _const_spec((1, D_MODEL)),
        ],
        out_specs=pl.BlockSpec((TM, D_MODEL), lambda i: (i, 0)),
        scratch_shapes=[pltpu.VMEM((TM, D_FF), BF16)],
        compiler_params=_cparams(("arbitrary",)),
        name="conv_ffn",
    )(x, x, x, mod, *weights, w["g_final"])


def _state_specs(nb, n_slots, n, width, slot, prev, sps):
    shape = jax.ShapeDtypeStruct((nb, n_slots, n, width), F32)
    if prev is None:
        return shape, pl.BlockSpec((sps, n_slots, n, width), lambda i: (i, 0, 0, 0))
    return shape, pl.BlockSpec((sps, 1, n, width), lambda i: (i, slot, 0, 0))


def _write_state(ref, j, slot, val):
    if ref.shape[1] == 1:
        ref[j, 0] = val
    else:
        for k in range(ref.shape[1]):
            ref[j, k] = val if k == slot else jnp.zeros_like(val)


def _even_body(*refs, n, tq, latent, slot, sps):
    if latent:
        (x_ref, xq_ref, mod_ref, g_ref, win_ref, gq_ref, gkv_ref, wuq_ref, wkc_ref, wuv_ref, wpool_ref,
         pscale_ref, cnt_ref, wout_ref, cos_ref, sp_ref, sn_ref, ctx_ref,
         o_ref, qs, ks, vs, xps, osc) = refs
        lat_ref = None
    else:
        (x_ref, mod_ref, g_ref, win_ref, gq_ref, gkv_ref, wuq_ref, wkc_ref, wuv_ref, wpool_ref,
         pscale_ref, cnt_ref, wout_ref) = refs[:13]
        o_ref, lat_ref, qs, ks, vs, xps, osc = refs[-7:]
    nt = n // TM
    nk = n + (PAST_LEN if latent else 0)
    npool = n + 2 * POOL_PAD
    m = mod_ref[0, 0]
    lane = lax.broadcasted_iota(jnp.int32, (TM, LANES), 1)
    lo = lax.broadcasted_iota(jnp.int32, (tq, LANES), 1) < V_DIM

    def store_v(v, rows):
        for p in range(MLA_HEADS // 2):
            vp = v[:, LANES * p:LANES * (p + 1)]
            ln = lane[:v.shape[0]]
            vs[2 * p, rows, :] = jnp.where(ln < V_DIM, vp, jnp.where(ln == V_DIM, 1.0, 0.0)).astype(BF16)
            vs[2 * p + 1, rows, :] = jnp.where(ln >= V_DIM, vp, jnp.where(ln == 0, 1.0, 0.0)).astype(BF16)

    def project(j, r0, x):
        h = (_rms(x, g_ref[0]) * (1.0 + m[1:2]) + m[0:1]).astype(BF16)
        z = _dot(h, win_ref[0])
        qn = _rms(z[:, :Q_LORA], gq_ref[0]).astype(BF16)
        q = _dot(qn, wuq_ref[0])
        latc = _rms(z[:, Q_LORA:Q_LORA + KV_LORA], gkv_ref[0])
        krp = z[:, 896:1024]
        if latent:
            cos, sp, sn = cos_ref[...], sp_ref[...], sn_ref[...]
            krp = _rope(krp, cos, sp, sn, 8)
        if lat_ref is not None:
            kr0 = pltpu.roll(krp, LANES - 64, 1)
            _write_state(lat_ref, j, slot, jnp.concatenate([latc, kr0[:, :QK_ROPE]], axis=1))
        lb = jnp.concatenate([latc, krp], axis=1).astype(BF16)
        kc = _dot(lb, wkc_ref[0])
        v = _dot(lb[:, :KV_LORA], wuv_ref[0])
        for hd in range(MLA_HEADS):
            qh = q[:, LANES * hd:LANES * (hd + 1)]
            if latent:
                qh = _rope(qh, cos, sp, sn, 8)
            qs[hd, pl.ds(j * n + r0, TM), :] = (qh * (MLA_SCALE * LOG2E)).astype(BF16)
            ks[hd, pl.ds(j * nk + r0, TM), :] = kc[:, LANES * hd:LANES * (hd + 1)].astype(BF16)
        store_v(v, pl.ds(j * nk + r0, TM))
        xps[pl.ds(j * npool + POOL_PAD + r0, TM), :] = z[:, 640:896]

    def first(j):
        xps[j * npool:j * npool + POOL_PAD, :] = jnp.zeros((POOL_PAD, POOL_DIM), F32)
        xps[j * npool + n + POOL_PAD:(j + 1) * npool, :] = jnp.zeros((POOL_PAD, POOL_DIM), F32)
        if latent:
            cb = ctx_ref[0, 0]
            kcc = _dot(cb, wkc_ref[0])
            for hd in range(MLA_HEADS):
                ks[hd, n:n + PAST_LEN, :] = kcc[:, LANES * hd:LANES * (hd + 1)].astype(BF16)
            for half in range(PAST_LEN // TM):
                vc = _dot(cb[TM * half:TM * (half + 1), :KV_LORA], wuv_ref[0])
                store_v(vc, pl.ds(n + TM * half, TM))

    def attend(j, q0, xq, write):
        def head(hd, l_lane):
            sc = _dot_t(qs[hd, pl.ds(j * n + q0, tq), :], ks[hd, j * nk:(j + 1) * nk, :])
            e = jnp.exp2(sc - jnp.max(sc, axis=-1, keepdims=True))
            o = _dot(e.astype(BF16), vs[hd, j * nk:(j + 1) * nk, :])
            return o * (1.0 / o[:, l_lane:l_lane + 1])

        for p in range(MLA_HEADS // 2):
            osc[j, p] = jnp.where(lo, head(2 * p, V_DIM), head(2 * p + 1, 0)).astype(BF16)

        rows = tq + 2 * POOL_PAD
        a0 = xps[pl.ds(j * npool + q0, rows), :]
        s1 = a0 + pltpu.roll(a0, 1, 0)
        s2 = s1 + pltpu.roll(s1, 2, 0)
        s4 = s2 + pltpu.roll(s2, 4, 0)
        s8 = s4 + pltpu.roll(s4, 8, 0)
        pl_lane = lax.broadcasted_iota(jnp.int32, (rows, POOL_DIM), 1)
        win = jnp.where(pl_lane < 64, s1,
                        jnp.where(pl_lane < 128, pltpu.roll(s2, rows - 1, 0),
                                  jnp.where(pl_lane < 192, pltpu.roll(s4, rows - 3, 0),
                                            pltpu.roll(s8, rows - 7, 0))))
        pooled = win[POOL_PAD:POOL_PAD + tq] / cnt_ref[...] - a0[POOL_PAD:POOL_PAD + tq]
        ypool = _dot(pooled.astype(BF16), wpool_ref[0]) * pscale_ref[0]

        mix = jnp.concatenate([osc[j, p] for p in range(MLA_HEADS // 2)] + [ypool.astype(BF16)], axis=1)
        write(xq + m[2:3] * _dot(mix, wout_ref[0]))

    def write_rows(r0):
        def write(val):
            o_ref[r0:r0 + tq, :] = val
        return write

    if latent:
        s = pl.program_id(1)

        @pl.when(s < nt)
        def _():
            project(0, pl.multiple_of(s * TM, TM), x_ref[...])

            @pl.when(s == 0)
            def _():
                first(0)

        @pl.when(s >= nt)
        def _():
            attend(0, pl.multiple_of((s - nt) * tq, tq), xq_ref[...], write_rows(0))
    else:
        for j in range(sps):
            first(j)
            project(j, 0, x_ref[j * n:(j + 1) * n, :])
        for j in range(sps):
            attend(j, 0, x_ref[j * n:(j + 1) * n, :], write_rows(j * n))


def _even_mixer(x, mod, l, n, latent, g_mix, w, tables, ctx, n_slots=1, prev_lat=None):
    slot = l // 2
    t = x.shape[0]
    nb = t // n
    nt = n // TM
    tq = min(n, TQ_MLA)
    ntq = n // tq
    nk = n + (PAST_LEN if latent else 0)
    sps = 1 if latent else CTX_SEQS_PER_STEP
    names = ["w_in", "g_q", "g_kv", "w_uq", "w_kcat", "w_uv", "w_pool", "pool_scale"]
    weights = [_layer_spec(g_mix, l), *[_layer_spec(w[k], slot) for k in names]]
    out_shape = [jax.ShapeDtypeStruct((t, D_MODEL), F32)]
    aliases = {}
    if latent:
        def qmap(b, s):
            return (b * ntq + jnp.maximum(s - nt, 0), 0)
        tspec = pl.BlockSpec((TM, LANES), lambda b, s: (jnp.minimum(s, nt - 1), 0))
        grid = (nb, nt + ntq)
        in_specs = [
            pl.BlockSpec((TM, D_MODEL), lambda b, s: (b * nt + jnp.minimum(s, nt - 1), 0)),
            pl.BlockSpec((tq, D_MODEL), qmap),
            pl.BlockSpec((1, 1, N_MOD, D_MODEL), lambda b, s: (l, 1 + b, 0, 0)),
            *weights,
            pl.BlockSpec((tq, POOL_DIM), lambda b, s: (jnp.maximum(s - nt, 0), 0)),
            _layer_spec(w["w_out"], slot),
            tspec, tspec, tspec,
            pl.BlockSpec((1, 1, PAST_LEN, 384), lambda b, s: (b, slot, 0, 0)),
        ]
        args = [x, x, mod, g_mix, *[w[k] for k in names], tables["cnt"], w["w_out"],
                tables["cos"], tables["sin_prev"], tables["sin_next"], ctx]
        out_specs = [pl.BlockSpec((tq, D_MODEL), qmap)]
    else:
        grid = (nb // sps,)
        in_specs = [
            pl.BlockSpec((sps * n, D_MODEL), lambda i: (i, 0)),
            pl.BlockSpec((1, 1, N_MOD, D_MODEL), lambda i: (l, 0, 0, 0)),
            *weights,
            _const_spec((tq, POOL_DIM)),
            _layer_spec(w["w_out"], slot),
        ]
        args = [x, mod, g_mix, *[w[k] for k in names], tables["cnt"], w["w_out"]]
        out_specs = [pl.BlockSpec((sps * n, D_MODEL), lambda i: (i, 0))]
        shape, spec = _state_specs(nb, n_slots, n, KV_LORA + QK_ROPE, slot, prev_lat, sps)
        out_shape.append(shape)
        out_specs.append(spec)
        if prev_lat is not None:
            aliases = {len(args): 1}
            in_specs.append(pl.BlockSpec(memory_space=pl.ANY))
            args.append(prev_lat)
    scratch = [
        pltpu.VMEM((MLA_HEADS, sps * n, LANES), BF16),
        pltpu.VMEM((MLA_HEADS, sps * nk, LANES), BF16),
        pltpu.VMEM((MLA_HEADS, sps * nk, LANES), BF16),
        pltpu.VMEM((sps * (n + 2 * POOL_PAD), POOL_DIM), F32),
        pltpu.VMEM((sps, MLA_HEADS // 2, tq, LANES), BF16),
    ]
    return pl.pallas_call(
        functools.partial(_even_body, n=n, tq=tq, latent=latent, slot=slot, sps=sps),
        out_shape=out_shape,
        grid=grid,
        in_specs=in_specs,
        out_specs=out_specs,
        scratch_shapes=scratch,
        input_output_aliases=aliases,
        compiler_params=_cparams(("arbitrary",) * len(grid)),
        name="even_latent" if latent else "even_context",
    )(*args)


def _bias_body(rpb_ref, o_ref):
    qc = lax.broadcasted_iota(jnp.int32, (GRID_W, LANES), 0)
    kc = lax.broadcasted_iota(jnp.int32, (GRID_W, LANES), 1)
    c0 = jnp.clip(qc - NA_KW // 2, 0, GRID_W - NA_KW)
    inwin = (kc >= c0) & (kc < c0 + NA_KW)
    blocks = []
    for dr in range(2 * NA_KH - 1):
        row = jnp.broadcast_to(rpb_ref[0, dr:dr + 1, :], (GRID_W, LANES))
        toep = pltpu.roll(row, LANES - (NA_KW - 1), 1, stride=1, stride_axis=0)
        blocks.append(jnp.where(inwin, toep, NEG)[:, :GRID_W])
    for v in range(NA_KH):
        for i in range(NA_KH):
            o_ref[0, v, :, GRID_W * i:GRID_W * (i + 1)] = blocks[v + i]


def _na_bias(rpb):
    nh, ndr, nj = rpb.shape
    rpb = jnp.pad(rpb, ((0, 0), (0, 0), (0, LANES - nj)))
    return pl.pallas_call(
        _bias_body,
        out_shape=jax.ShapeDtypeStruct((nh, NA_KH, GRID_W, NA_KH * GRID_W), F32),
        grid=(nh,),
        in_specs=[pl.BlockSpec((1, ndr, LANES), lambda h: (h, 0, 0))],
        out_specs=pl.BlockSpec((1, NA_KH, GRID_W, NA_KH * GRID_W), lambda h: (h, 0, 0, 0)),
        compiler_params=_cparams(("arbitrary",)),
        name="na_bias",
    )(rpb)


def _softmax_pv(parts, extra=None):
    mx = functools.reduce(jnp.maximum, [jnp.max(sc, axis=-1, keepdims=True) for sc, _ in parts])
    if extra is not None:
        mx = jnp.maximum(mx, extra)
    l = jnp.exp(extra - mx) if extra is not None else 0.0
    acc = None
    for sc, val in parts:
        e = jnp.exp(sc - mx)
        l = l + jnp.sum(e, axis=-1, keepdims=True)
        o = val(e.astype(BF16)) if callable(val) else _dot(e.astype(BF16), val)
        acc = o if acc is None else acc + o
    return acc / l


def _odd_body(*refs, n, latent, slot, sps):
    if latent:
        (x_ref, mod_ref, g_ref, win_ref, sink_ref, wout_ref, cos_ref, sp_ref, sn_ref, bias_ref,
         cna_ref, csw_ref,
         o_ref, qna, kna, vna, qsw, ksw, vsw, osc) = refs
        nakv_ref = swkv_ref = None
    else:
        x_ref, mod_ref, g_ref, win_ref, sink_ref, wout_ref = refs[:6]
        o_ref, nakv_ref, swkv_ref, qna, kna, vna, qsw, ksw, vsw, osc = refs[-10:]
    nt = n // TM
    nk = n + (PAST_LEN if latent else 0)
    nsw = n + (2 * SWA_WINDOW + PAST_LEN if latent else 0)
    m = mod_ref[0, 0]
    sw_off = SWA_WINDOW if latent else 0
    lane = lax.broadcasted_iota(jnp.int32, (TM, LANES), 1)
    lo = lane < HEAD_DIM

    def project(j, r0, x):
        h = (_rms(x, g_ref[0]) * (1.0 + m[1:2]) + m[0:1]).astype(BF16)
        z_sw = _dot(h, win_ref[0, :, 1536:2304])
        z = jnp.concatenate([_dot(h, win_ref[0, :, :1536]), z_sw], axis=1)
        if nakv_ref is not None:
            _write_state(nakv_ref, j, slot, z[:, 512:1536])
            _write_state(swkv_ref, j, slot, z[:, 2048:2304])
        scale = HEAD_DIM ** -0.5
        qrows, krows = pl.ds(j * n + r0, TM), pl.ds(j * nk + r0, TM)
        for p in range(NA_HEADS // 2):
            qp = z[:, LANES * p:LANES * (p + 1)] * scale
            qna[2 * p, qrows, :] = jnp.where(lo, qp, 0.0).astype(BF16)
            qna[2 * p + 1, qrows, :] = jnp.where(lo, 0.0, qp).astype(BF16)
            kna[p, krows, :] = z[:, 512 + LANES * p:512 + LANES * (p + 1)].astype(BF16)
            vna[p, krows, :] = z[:, 1024 + LANES * p:1024 + LANES * (p + 1)].astype(BF16)
        if latent:
            cos, sp, sn = cos_ref[...], sp_ref[...], sn_ref[...]
        for p in range(SWA_HEADS // 2):
            qp = z[:, 1536 + LANES * p:1536 + LANES * (p + 1)]
            if latent:
                qp = _rope(qp, cos, sp, sn, 16)
            qp = qp * scale
            qsw[2 * p, qrows, :] = jnp.where(lo, qp, 0.0).astype(BF16)
            qsw[2 * p + 1, qrows, :] = jnp.where(lo, 0.0, qp).astype(BF16)
        k = z[:, 2048:2176]
        if latent:
            k = _rope(k, cos, sp, sn, 16)
        v = z[:, 2176:2304]
        ksw_, vsw_ = pltpu.roll(k, HEAD_DIM, 1), pltpu.roll(v, HEAD_DIM, 1)
        swrows = pl.ds(j * nsw + sw_off + r0, TM)
        ksw[0, swrows, :] = jnp.where(lo, k, ksw_).astype(BF16)
        ksw[1, swrows, :] = jnp.where(lo, ksw_, k).astype(BF16)
        vsw[0, swrows, :] = jnp.where(lo, v, vsw_).astype(BF16)
        vsw[1, swrows, :] = jnp.where(lo, vsw_, v).astype(BF16)

    def first():
        cna = cna_ref[0, 0]
        for p in range(NA_HEADS // 2):
            kna[p, n:n + PAST_LEN, :] = cna[:, LANES * p:LANES * (p + 1)]
            vna[p, n:n + PAST_LEN, :] = cna[:, 512 + LANES * p:512 + LANES * (p + 1)]
        csw = csw_ref[0, 0]
        zpad = jnp.zeros((SWA_WINDOW, LANES), BF16)
        for kv in range(SWA_KV_HEADS):
            ksw[kv, 0:SWA_WINDOW, :] = zpad
            vsw[kv, 0:SWA_WINDOW, :] = zpad
            ksw[kv, n + SWA_WINDOW:n + 2 * SWA_WINDOW, :] = zpad
            vsw[kv, n + SWA_WINDOW:n + 2 * SWA_WINDOW, :] = zpad
            ksw[kv, n + 2 * SWA_WINDOW:n + 2 * SWA_WINDOW + PAST_LEN, :] = csw[:, LANES * kv:LANES * (kv + 1)]
            vsw[kv, n + 2 * SWA_WINDOW:n + 2 * SWA_WINDOW + PAST_LEN, :] = csw[:, 256 + LANES * kv:256 + LANES * (kv + 1)]

    def attend(j, ti, xq, write):
        q0 = ti * TM if isinstance(ti, int) else pl.multiple_of(ti * TM, TM)
        rows_per_tile = TM // GRID_W
        nrows = n // GRID_W
        nloc = NA_KH * GRID_W

        def na_pair(p):
            if latent:
                blk = 2 * GRID_W
                qst = jnp.concatenate(
                    [qna[2 * p + a, pl.ds(j * n + q0 + GRID_W * r, GRID_W), :]
                     for r in range(rows_per_tile) for a in (0, 1)], axis=0)
                s_loc, starts = [], []
                for r in range(rows_per_tile):
                    row = ti * rows_per_tile + r
                    rs = jnp.clip(row - NA_KH // 2, 0, nrows - NA_KH)
                    k0 = pl.multiple_of(rs * GRID_W, GRID_W)
                    starts.append(k0)
                    var = rs - row + NA_KH - 1
                    bias = jnp.concatenate([bias_ref[0, 2 * p, var], bias_ref[0, 2 * p + 1, var]], axis=0)
                    s_loc.append(_dot_t(qst[blk * r:blk * (r + 1)], kna[p, pl.ds(k0, nloc), :]) + bias)
                s_loc = jnp.concatenate(s_loc, axis=0)
                s_ctx = _dot_t(qst, kna[p, pl.ds(n, PAST_LEN), :])

                def pv_loc(e):
                    return jnp.concatenate(
                        [_dot(e[blk * r:blk * (r + 1)], vna[p, pl.ds(starts[r], nloc), :])
                         for r in range(rows_per_tile)], axis=0)

                o = _softmax_pv([(s_loc, pv_loc), (s_ctx, vna[p, pl.ds(n, PAST_LEN), :])])
                outs = [jnp.concatenate([o[blk * r + GRID_W * a:blk * r + GRID_W * (a + 1)]
                                         for r in range(rows_per_tile)], axis=0) for a in (0, 1)]
            else:
                qst = jnp.concatenate([qna[2 * p + a, pl.ds(j * n + q0, TM), :] for a in (0, 1)], axis=0)
                o = _softmax_pv([(_dot_t(qst, kna[p, j * nk:(j + 1) * nk, :]), vna[p, j * nk:(j + 1) * nk, :])])
                outs = [o[:TM], o[TM:]]
            osc[j, p] = jnp.where(lo, outs[0], outs[1]).astype(BF16)

        if latent:
            def na_step(i, carry):
                na_pair(2 * i)
                na_pair(2 * i + 1)
                return carry
            lax.fori_loop(0, NA_HEADS // 4, na_step, 0)
        else:
            for p in range(NA_HEADS // 2):
                na_pair(p)

        group = SWA_HEADS // SWA_KV_HEADS
        stack = SWA_STACK
        for c in range(SWA_HEADS // stack):
            kv = c * stack // group
            qst = jnp.concatenate([qsw[stack * c + g, pl.ds(j * n + q0, TM), :] for g in range(stack)], axis=0)
            sink = jnp.concatenate(
                [jnp.full((TM, 1), sink_ref[slot, stack * c + g], F32) for g in range(stack)], axis=0)
            if latent:
                span = TM + 2 * SWA_WINDOW
                s_loc = _dot_t(qst, ksw[kv, pl.ds(q0, span), :])
                a_i = lax.broadcasted_iota(jnp.int32, (TM, span), 0)
                b_i = lax.broadcasted_iota(jnp.int32, (TM, span), 1)
                key = q0 - SWA_WINDOW + b_i
                ok = (b_i >= a_i) & (b_i <= a_i + 2 * SWA_WINDOW) & (key >= 0) & (key < n)
                s_loc = jnp.concatenate(
                    [jnp.where(ok, s_loc[TM * g:TM * (g + 1)], NEG) for g in range(stack)], axis=0)
                c0 = n + 2 * SWA_WINDOW
                s_ctx = _dot_t(qst, ksw[kv, pl.ds(c0, PAST_LEN), :])
                o = _softmax_pv([(s_loc, vsw[kv, pl.ds(q0, span), :]),
                                 (s_ctx, vsw[kv, pl.ds(c0, PAST_LEN), :])], extra=sink)
            else:
                o = _softmax_pv([(_dot_t(qst, ksw[kv, j * nsw:(j + 1) * nsw, :]),
                                  vsw[kv, j * nsw:(j + 1) * nsw, :])], extra=sink)
            for u in range(stack // 2):
                osc[j, NA_HEADS // 2 + (stack // 2) * c + u] = jnp.where(
                    lo, o[TM * 2 * u:TM * (2 * u + 1)], o[TM * (2 * u + 1):TM * (2 * u + 2)]).astype(BF16)

        mix = jnp.concatenate([osc[j, p] for p in range(8)], axis=1)
        write(xq + m[2:3] * _dot(mix, wout_ref[0]))

    if latent:
        s = pl.program_id(1)

        @pl.when(s < nt)
        def _():
            project(0, pl.multiple_of(s * TM, TM), x_ref[...])

            @pl.when(s == 0)
            def _():
                first()

        @pl.when(s >= nt)
        def _():
            def write(val):
                o_ref[...] = val
            attend(0, s - nt, x_ref[...], write)
    else:
        for j in range(sps):
            project(j, 0, x_ref[j * n:(j + 1) * n, :])

        @pl.when(pl.program_id(0) >= 0)
        def _():
            for j in range(sps):
                def write(val, j=j):
                    o_ref[j * n:(j + 1) * n, :] = val
                attend(j, 0, x_ref[j * n:(j + 1) * n, :], write)


def _odd_mixer(x, mod, l, n, latent, g_mix, w, tables, bias, cna, csw, n_slots=1, prev=None):
    slot = l // 2
    t = x.shape[0]
    nb = t // n
    nt = n // TM
    nk = n + (PAST_LEN if latent else 0)
    nsw = n + (2 * SWA_WINDOW + PAST_LEN if latent else 0)
    sps = 1 if latent else CTX_SEQS_PER_STEP
    weights = [_layer_spec(g_mix, l), _layer_spec(w["w_in"], slot), pl.BlockSpec(memory_space=pltpu.SMEM),
               _layer_spec(w["w_out"], slot)]
    args = [x, mod, g_mix, w["w_in"], w["sink"], w["w_out"]]
    out_shape = [jax.ShapeDtypeStruct((t, D_MODEL), F32)]
    aliases = {}
    if latent:
        grid = (nb, 2 * nt)
        tspec = pl.BlockSpec((TM, LANES), lambda b, s: (jnp.minimum(s, nt - 1), 0))
        in_specs = [pl.BlockSpec((TM, D_MODEL), lambda b, s: (b * nt + s % nt, 0)),
                    pl.BlockSpec((1, 1, N_MOD, D_MODEL), lambda b, s: (l, 1 + b, 0, 0)),
                    *weights, tspec, tspec, tspec, _layer_spec(bias, slot),
                    pl.BlockSpec((1, 1, PAST_LEN, 1024), lambda b, s: (b, slot, 0, 0)),
                    pl.BlockSpec((1, 1, PAST_LEN, 512), lambda b, s: (b, slot, 0, 0))]
        args += [tables["cos"], tables["sin_prev"], tables["sin_next"], bias, cna, csw]
        out_specs = [pl.BlockSpec((TM, D_MODEL), lambda b, s: (b * nt + jnp.maximum(s - nt, 0), 0))]
    else:
        grid = (nb // sps,)
        in_specs = [pl.BlockSpec((sps * n, D_MODEL), lambda i: (i, 0)),
                    pl.BlockSpec((1, 1, N_MOD, D_MODEL), lambda i: (l, 0, 0, 0)), *weights]
        out_specs = [pl.BlockSpec((sps * n, D_MODEL), lambda i: (i, 0))]
        for k, width in enumerate((2 * NA_HEADS * HEAD_DIM, 2 * SWA_KV_HEADS * HEAD_DIM)):
            shape, spec = _state_specs(nb, n_slots, n, width, slot, prev, sps)
            out_shape.append(shape)
            out_specs.append(spec)
            if prev is not None:
                aliases[len(args)] = 1 + k
                in_specs.append(pl.BlockSpec(memory_space=pl.ANY))
                args.append(prev[k])
    scratch = [
        pltpu.VMEM((NA_HEADS, sps * n, LANES), BF16),
        pltpu.VMEM((NA_HEADS // 2, sps * nk, LANES), BF16),
        pltpu.VMEM((NA_HEADS // 2, sps * nk, LANES), BF16),
        pltpu.VMEM((SWA_HEADS, sps * n, LANES), BF16),
        pltpu.VMEM((SWA_KV_HEADS, sps * nsw, LANES), BF16),
        pltpu.VMEM((SWA_KV_HEADS, sps * nsw, LANES), BF16),
        pltpu.VMEM((sps, 8, TM, LANES), BF16),
    ]
    return pl.pallas_call(
        functools.partial(_odd_body, n=n, latent=latent, slot=slot, sps=sps),
        out_shape=out_shape,
        grid=grid,
        in_specs=in_specs,
        out_specs=out_specs,
        scratch_shapes=scratch,
        input_output_aliases=aliases,
        compiler_params=_cparams(("arbitrary",) * len(grid)),
        name="odd_latent" if latent else "odd_context",
    )(*args)


def _rope_tables(n, rot, lane0, period):
    half = rot // 2
    nf = half // 2
    t = np.arange(n)
    freqs = (ROPE_BASE ** (-np.arange(nf, dtype=np.float32) / nf)).astype(np.float32)
    cos = np.ones((n, period), np.float32)
    sp = np.zeros((n, period), np.float32)
    sn = np.zeros((n, period), np.float32)
    for k, pos in enumerate((t // GRID_W, t % GRID_W)):
        ang = pos.astype(np.float32)[:, None] * freqs
        c, s_ = np.cos(ang), np.sin(ang)
        a = lane0 + k * half
        cos[:, a:a + nf] = c
        cos[:, a + nf:a + half] = c
        sn[:, a:a + nf] = -s_
        sp[:, a + nf:a + half] = s_
    reps = LANES // period
    return {"cos": jnp.asarray(np.tile(cos, (1, reps))), "sin_prev": jnp.asarray(np.tile(sp, (1, reps))),
            "sin_next": jnp.asarray(np.tile(sn, (1, reps)))}


def _pool_counts(n):
    t = np.arange(n)
    cols = []
    for wdw in POOL_WINDOWS:
        lo = np.clip(t - wdw // 2, 0, n)
        hi = np.clip(t + wdw // 2, 0, n)
        cols.append(np.repeat((hi - lo).astype(np.float32)[:, None], POOL_GROUP, axis=1))
    return jnp.asarray(np.concatenate(cols, axis=1))


def _even_weights(w_in, g_q, g_kv, w_uq, w_ukv, w_pool, pool_scale, w_out):
    ne = w_in.shape[0]
    b_, c_ = Q_LORA + KV_LORA, Q_LORA + KV_LORA + QK_ROPE
    krpad = jnp.pad(w_in[..., b_:c_], ((0, 0), (0, 0), (64, LANES - 64 - QK_ROPE)))
    w_in_p = jnp.concatenate([w_in[..., :b_], w_in[..., c_:], krpad], axis=-1)
    uq = w_uq.reshape(ne, Q_LORA, MLA_HEADS, QK_NOPE + QK_ROPE)
    uq = jnp.pad(uq, ((0, 0),) * 3 + ((0, LANES - QK_NOPE - QK_ROPE),)).reshape(ne, Q_LORA, MLA_HEADS * LANES)
    ukv = w_ukv.reshape(ne, KV_LORA, MLA_HEADS, QK_NOPE + V_DIM)
    uk = jnp.pad(ukv[..., :QK_NOPE], ((0, 0),) * 3 + ((0, LANES - QK_NOPE),))
    sel = np.zeros((LANES, MLA_HEADS, LANES), np.float32)
    sel[64 + np.arange(QK_ROPE), :, 64 + np.arange(QK_ROPE)] = 1.0
    sel = jnp.broadcast_to(jnp.asarray(sel), (ne,) + sel.shape)
    w_kcat = jnp.concatenate([uk, sel], axis=1).reshape(ne, KV_LORA + LANES, MLA_HEADS * LANES)
    w_uv = ukv[..., QK_NOPE:].reshape(ne, KV_LORA, MLA_HEADS * V_DIM)
    eye = jnp.asarray(np.eye(len(POOL_WINDOWS), dtype=np.float32))
    w_pool_bd = (w_pool[:, :, :, None, :] * eye[None, :, None, :, None]).reshape(ne, POOL_DIM, POOL_DIM)
    return {"w_in": w_in_p.astype(BF16), "g_q": g_q.reshape(ne, 1, -1), "g_kv": g_kv.reshape(ne, 1, -1),
            "w_uq": uq.astype(BF16), "w_kcat": w_kcat.astype(BF16), "w_uv": w_uv.astype(BF16),
            "w_pool": w_pool_bd.astype(BF16), "pool_scale": pool_scale.reshape(ne, 1, -1),
            "w_out": w_out.astype(BF16)}


def kernel(x_prompt, x_sample, cache_mla_latent, cache_na_kv, cache_swa_kv, c, c_ctx, w_mod, b_mod, norm_mix, norm_ffn, norm_final, w_in_even, mla_q_norm, mla_kv_norm, w_uq, w_ukv, w_pool, pool_scale, w_out_even, w_in_odd, na_rpb, swa_sink, w_out_odd, w_up, conv_w, conv_b, w_down):
    nbp, n_p, _ = x_prompt.shape
    nbs, n_s, _ = x_sample.shape
    depth = w_mod.shape[0]
    xp = x_prompt.reshape(nbp * n_p, D_MODEL)
    xs = x_sample.reshape(nbs * n_s, D_MODEL)

    c_all = jnp.zeros((8, D_MODEL), F32).at[0].set(c_ctx).at[1:1 + nbs].set(c)
    mod = _adaln(c_all, w_mod, b_mod).reshape(depth, 8, N_MOD, D_MODEL)

    tab_mla = _rope_tables(n_s, QK_ROPE, 64, LANES)
    tab_swa = _rope_tables(n_s, HEAD_DIM, 0, HEAD_DIM)
    cnt_p, cnt_s = _pool_counts(n_p), _pool_counts(n_s)

    bias_all = _na_bias(na_rpb.reshape((-1,) + na_rpb.shape[2:])).reshape(
        na_rpb.shape[:2] + (NA_KH, GRID_W, NA_KH * GRID_W))

    n_even, n_odd = (depth + 1) // 2, depth // 2
    g_mix = norm_mix.reshape(depth, 1, D_MODEL)
    w_even = _even_weights(w_in_even, mla_q_norm, mla_kv_norm, w_uq, w_ukv, w_pool, pool_scale, w_out_even)
    w_odd = {"w_in": w_in_odd.astype(BF16), "w_out": w_out_odd.astype(BF16), "sink": swa_sink}
    w_ffn = {"g_ffn": norm_ffn.reshape(depth, 1, D_MODEL), "w_up": w_up.astype(BF16), "conv_w": conv_w,
             "conv_b": conv_b.reshape(depth, 1, -1), "w_down": w_down.astype(BF16),
             "g_final": norm_final.reshape(1, -1)}
    zeros = functools.partial(jnp.zeros, dtype=F32)
    lead = cache_mla_latent.shape[:3]
    ctx = jnp.concatenate([cache_mla_latent[..., :KV_LORA], zeros(lead + (64,)),
                           cache_mla_latent[..., KV_LORA:], zeros(lead + (32,))], axis=-1).astype(BF16)
    cna = cache_na_kv.reshape(nbs, -1, PAST_LEN, 2 * NA_HEADS * HEAD_DIM).astype(BF16)
    csw = cache_swa_kv.reshape(nbs, -1, PAST_LEN, 2 * SWA_KV_HEADS, HEAD_DIM)
    csw = jnp.concatenate([csw, csw], axis=-1).reshape(nbs, -1, PAST_LEN, 512).astype(BF16)

    lat = kv = None
    for l in range(depth):
        if l % 2 == 0:
            xp, lat = _even_mixer(xp, mod, l, n_p, False, g_mix, w_even, {"cnt": cnt_p}, None,
                                  n_slots=n_even, prev_lat=lat)
            (xs,) = _even_mixer(xs, mod, l, n_s, True, g_mix, w_even, dict(tab_mla, cnt=cnt_s), ctx)
        else:
            xp, *kv = _odd_mixer(xp, mod, l, n_p, False, g_mix, w_odd, None, None, None, None,
                                 n_slots=n_odd, prev=kv)
            (xs,) = _odd_mixer(xs, mod, l, n_s, True, g_mix, w_odd, tab_swa, bias_all, cna, csw)
        xp = _ffn(xp, n_p, False, mod, l, w_ffn, l == depth - 1)
        xs = _ffn(xs, n_s, True, mod, l, w_ffn, l == depth - 1)

    return (xp.reshape(nbp, n_p, D_MODEL), xs.reshape(nbs, n_s, D_MODEL), lat,
            kv[0].reshape(nbp, n_odd, n_p, 2, NA_HEADS, HEAD_DIM),
            kv[1].reshape(nbp, n_odd, n_p, 2, SWA_KV_HEADS, HEAD_DIM))
```

```python
import functools

import numpy as np
import jax
import jax.numpy as jnp
from jax import lax
from jax.experimental import pallas as pl
from jax.experimental.pallas import tpu as pltpu

F32 = jnp.float32
BF16 = jnp.bfloat16

D_MODEL = 1024
GRID_W = 64
N_MOD = 6
EPS = 1e-6
ROPE_BASE = 10000.0
NEG = -1e30
MLA_HEADS = 12
Q_LORA = 384
KV_LORA = 256
QK_NOPE = 64
QK_ROPE = 32
V_DIM = 64
MLA_SCALE = (QK_NOPE + QK_ROPE) ** -0.5
LOG2E = 1.4426950408889634
POOL_WINDOWS = (2, 4, 8, 16)
POOL_GROUP = 64
POOL_DIM = 256
NA_HEADS = 8
NA_KH = 8
NA_KW = 16
SWA_HEADS = 8
SWA_KV_HEADS = 2
SWA_WINDOW = 128
HEAD_DIM = 64
D_FF = 2816
PAST_LEN = 512

LANES = 128
TM = 256
TQ_MLA = 512
SWA_STACK = 2
CTX_SEQS_PER_STEP = 1
FF_CHUNK = 256
FFN_TILES = 2
N_FF_CHUNKS = D_FF // FF_CHUNK
POOL_PAD = 16
VMEM_LIMIT = 60000 * 1024


def _cparams(sem):
    return pltpu.CompilerParams(dimension_semantics=sem, vmem_limit_bytes=VMEM_LIMIT)


def _rms(x, g):
    return x * lax.rsqrt(jnp.mean(x * x, axis=-1, keepdims=True) + EPS) * g


def _silu(x):
    return x * (1.0 / (1.0 + jnp.exp(-x)))


def _dot(a, b):
    return jnp.dot(a, b, preferred_element_type=F32)


def _dot_t(a, b):
    return lax.dot_general(a, b, (((1,), (1,)), ((), ())), preferred_element_type=F32)


def _rope(x, cos, sin_prev, sin_next, shift):
    w = x.shape[-1]
    return x * cos + pltpu.roll(x, shift, 1) * sin_prev + pltpu.roll(x, w - shift, 1) * sin_next


def _const_spec(shape):
    nd = len(shape)
    return pl.BlockSpec(shape, lambda *_: (0,) * nd, pipeline_mode=pl.Buffered(1))


def _layer_spec(arr, l):
    shape = (1,) + arr.shape[1:]
    tail = (0,) * (arr.ndim - 1)
    return pl.BlockSpec(shape, lambda *_: (l,) + tail, pipeline_mode=pl.Buffered(1))


def _adaln_body(c_ref, w_ref, b_ref, o_ref):
    a = _silu(c_ref[...]).astype(BF16)
    o_ref[0] = _dot(a, w_ref[0].astype(BF16)) + b_ref[0]


def _adaln(c_all, w_mod, b_mod):
    depth, _, width = w_mod.shape
    tn = 1536
    return pl.pallas_call(
        _adaln_body,
        out_shape=jax.ShapeDtypeStruct((depth, 8, width), F32),
        grid=(depth, width // tn),
        in_specs=[
            pl.BlockSpec((8, D_MODEL), lambda l, j: (0, 0)),
            pl.BlockSpec((1, D_MODEL, tn), lambda l, j: (l, 0, j)),
            pl.BlockSpec((1, 1, tn), lambda l, j: (l, 0, j)),
        ],
        out_specs=pl.BlockSpec((1, 8, tn), lambda l, j: (l, 0, j)),
        compiler_params=_cparams(("arbitrary", "arbitrary")),
        name="adaln",
    )(c_all, w_mod, b_mod.reshape(depth, 1, width))


def _ffn_body(x_ref, x_prev_ref, x_next_ref, mod_ref, g_ref, wup_ref, cw_ref, cb_ref, wdn_ref, gf_ref,
              o_ref, act_ref, *, n, final):
    i = pl.program_id(0)
    m = mod_ref[0, 0]
    shift, scale, gate = m[3:4], m[4:5], m[5:6]
    g = g_ref[0]

    def hn(x):
        return _rms(x, g) * (1.0 + scale) + shift

    for k in range(FFN_TILES):
        x = x_ref[k * TM:(k + 1) * TM, :]
        prev = x_prev_ref[...] if k == 0 else x_ref[k * TM - 8:k * TM, :]
        nxt = x_next_ref[...] if k == FFN_TILES - 1 else x_ref[(k + 1) * TM:(k + 1) * TM + 8, :]
        tile_in_seq = (i * FFN_TILES + k) % (n // TM)
        keep_prev = jnp.where(tile_in_seq == 0, 0.0, 1.0)
        keep_next = jnp.where(tile_in_seq == n // TM - 1, 0.0, 1.0)
        hext = jnp.concatenate([hn(prev) * keep_prev, hn(x), hn(nxt) * keep_next], axis=0).astype(BF16)

        def conv(c0, hext=hext):
            u = _dot(hext, wup_ref[0, :, c0:c0 + FF_CHUNK])
            w = cw_ref[0, :, c0:c0 + FF_CHUNK]
            return (pltpu.roll(u, 1, 0)[8:8 + TM] * w[0:1] + u[8:8 + TM] * w[1:2]
                    + pltpu.roll(u, TM + 15, 0)[8:8 + TM] * w[2:3] + cb_ref[0, :, c0:c0 + FF_CHUNK])

        for j in range(N_FF_CHUNKS):
            act_ref[k, :, j * FF_CHUNK:(j + 1) * FF_CHUNK] = (
                _silu(conv(D_FF + j * FF_CHUNK)) * conv(j * FF_CHUNK)).astype(BF16)
        y = x + gate * _dot(act_ref[k], wdn_ref[0])
        if final:
            y = _rms(y, gf_ref[...])
        o_ref[k * TM:(k + 1) * TM, :] = y


def _ffn(x, n, latent, mod, l, w, final):
    t = x.shape[0]
    nblk8 = t // 8
    rows = FFN_TILES * TM

    def gmap(i):
        return (l, 1 + i // (n // rows) if latent else 0, 0, 0)

    weights = [w["g_ffn"], w["w_up"], w["conv_w"], w["conv_b"], w["w_down"]]
    return pl.pallas_call(
        functools.partial(_ffn_body, n=n, final=final),
        out_shape=jax.ShapeDtypeStruct((t, D_MODEL), F32),
        grid=(t // rows,),
        in_specs=[
            pl.BlockSpec((rows, D_MODEL), lambda i: (i, 0)),
            pl.BlockSpec((8, D_MODEL), lambda i: (jnp.maximum(i * (rows // 8) - 1, 0), 0)),
            pl.BlockSpec((8, D_MODEL), lambda i: (jnp.minimum((i + 1) * (rows // 8), nblk8 - 1), 0)),
            pl.BlockSpec((1, 1, N_MOD, D_MODEL), gmap),
            *[_layer_spec(a, l) for a in weights],
            _const_spec((1, D_MODEL)),
        ],
        out_specs=pl.BlockSpec((rows, D_MODEL), lambda i: (i, 0)),
        scratch_shapes=[pltpu.VMEM((FFN_TILES, TM, D_FF), BF16)],
        compiler_params=_cparams(("arbitrary",)),
        name="conv_ffn",
    )(x, x, x, mod, *weights, w["g_final"])


def _state_specs(nb, n_slots, n, width, slot, prev, sps):
    shape = jax.ShapeDtypeStruct((nb, n_slots, n, width), F32)
    if prev is None:
        return shape, pl.BlockSpec((sps, n_slots, n, width), lambda i: (i, 0, 0, 0))
    return shape, pl.BlockSpec((sps, 1, n, width), lambda i: (i, slot, 0, 0))


def _write_state(ref, j, slot, val):
    if ref.shape[1] == 1:
        ref[j, 0] = val
    else:
        for k in range(ref.shape[1]):
            ref[j, k] = val if k == slot else jnp.zeros_like(val)


def _even_body(*refs, n, tq, latent, slot, sps):
    if latent:
        (x_ref, xq_ref, mod_ref, g_ref, win_ref, gq_ref, gkv_ref, wuq_ref, wkc_ref, wuv_ref, wpool_ref,
         pscale_ref, cnt_ref, wout_ref, cos_ref, sp_ref, sn_ref, ctx_ref,
         o_ref, qs, ks, vs, xps, osc) = refs
        lat_ref = None
    else:
        (x_ref, mod_ref, g_ref, win_ref, gq_ref, gkv_ref, wuq_ref, wkc_ref, wuv_ref, wpool_ref,
         pscale_ref, cnt_ref, wout_ref) = refs[:13]
        o_ref, lat_ref, qs, ks, vs, xps, osc = refs[-7:]
    nt = n // TM
    nk = n + (PAST_LEN if latent else 0)
    npool = n + 2 * POOL_PAD
    m = mod_ref[0, 0]
    lane = lax.broadcasted_iota(jnp.int32, (TM, LANES), 1)
    lo = lax.broadcasted_iota(jnp.int32, (tq, LANES), 1) < V_DIM

    def store_v(v, rows):
        for p in range(MLA_HEADS // 2):
            vp = v[:, LANES * p:LANES * (p + 1)]
            ln = lane[:v.shape[0]]
            vs[2 * p, rows, :] = jnp.where(ln < V_DIM, vp, jnp.where(ln == V_DIM, 1.0, 0.0)).astype(BF16)
            vs[2 * p + 1, rows, :] = jnp.where(ln >= V_DIM, vp, jnp.where(ln == 0, 1.0, 0.0)).astype(BF16)

    def project(j, r0, x):
        h = (_rms(x, g_ref[0]) * (1.0 + m[1:2]) + m[0:1]).astype(BF16)
        z = _dot(h, win_ref[0])
        qn = _rms(z[:, :Q_LORA], gq_ref[0]).astype(BF16)
        q = _dot(qn, wuq_ref[0])
        latc = _rms(z[:, Q_LORA:Q_LORA + KV_LORA], gkv_ref[0])
        krp = z[:, 896:1024]
        if latent:
            cos, sp, sn = cos_ref[...], sp_ref[...], sn_ref[...]
            krp = _rope(krp, cos, sp, sn, 8)
        if lat_ref is not None:
            kr0 = pltpu.roll(krp, LANES - 64, 1)
            _write_state(lat_ref, j, slot, jnp.concatenate([latc, kr0[:, :QK_ROPE]], axis=1))
        lb = jnp.concatenate([latc, krp], axis=1).astype(BF16)
        kc = _dot(lb, wkc_ref[0])
        v = _dot(lb[:, :KV_LORA], wuv_ref[0])
        for hd in range(MLA_HEADS):
            qh = q[:, LANES * hd:LANES * (hd + 1)]
            if latent:
                qh = _rope(qh, cos, sp, sn, 8)
            qs[hd, pl.ds(j * n + r0, TM), :] = (qh * (MLA_SCALE * LOG2E)).astype(BF16)
            ks[hd, pl.ds(j * nk + r0, TM), :] = kc[:, LANES * hd:LANES * (hd + 1)].astype(BF16)
        store_v(v, pl.ds(j * nk + r0, TM))
        xps[pl.ds(j * npool + POOL_PAD + r0, TM), :] = z[:, 640:896]

    def first(j):
        xps[j * npool:j * npool + POOL_PAD, :] = jnp.zeros((POOL_PAD, POOL_DIM), F32)
        xps[j * npool + n + POOL_PAD:(j + 1) * npool, :] = jnp.zeros((POOL_PAD, POOL_DIM), F32)
        if latent:
            cb = ctx_ref[0, 0]
            kcc = _dot(cb, wkc_ref[0])
            for hd in range(MLA_HEADS):
                ks[hd, n:n + PAST_LEN, :] = kcc[:, LANES * hd:LANES * (hd + 1)].astype(BF16)
            for half in range(PAST_LEN // TM):
                vc = _dot(cb[TM * half:TM * (half + 1), :KV_LORA], wuv_ref[0])
                store_v(vc, pl.ds(n + TM * half, TM))

    def attend(j, q0, xq, write):
        def head(hd, l_lane):
            sc = _dot_t(qs[hd, pl.ds(j * n + q0, tq), :], ks[hd, j * nk:(j + 1) * nk, :])
            e = jnp.exp2(sc - jnp.max(sc, axis=-1, keepdims=True))
            o = _dot(e.astype(BF16), vs[hd, j * nk:(j + 1) * nk, :])
            return o * (1.0 / o[:, l_lane:l_lane + 1])

        for p in range(MLA_HEADS // 2):
            osc[j, p] = jnp.where(lo, head(2 * p, V_DIM), head(2 * p + 1, 0)).astype(BF16)

        rows = tq + 2 * POOL_PAD
        a0 = xps[pl.ds(j * npool + q0, rows), :]
        s1 = a0 + pltpu.roll(a0, 1, 0)
        s2 = s1 + pltpu.roll(s1, 2, 0)
        s4 = s2 + pltpu.roll(s2, 4, 0)
        s8 = s4 + pltpu.roll(s4, 8, 0)
        pl_lane = lax.broadcasted_iota(jnp.int32, (rows, POOL_DIM), 1)
        win = jnp.where(pl_lane < 64, s1,
                        jnp.where(pl_lane < 128, pltpu.roll(s2, rows - 1, 0),
                                  jnp.where(pl_lane < 192, pltpu.roll(s4, rows - 3, 0),
                                            pltpu.roll(s8, rows - 7, 0))))
        pooled = win[POOL_PAD:POOL_PAD + tq] / cnt_ref[...] - a0[POOL_PAD:POOL_PAD + tq]
        ypool = _dot(pooled.astype(BF16), wpool_ref[0]) * pscale_ref[0]

        mix = jnp.concatenate([osc[j, p] for p in range(MLA_HEADS // 2)] + [ypool.astype(BF16)], axis=1)
        write(xq + m[2:3] * _dot(mix, wout_ref[0]))

    def write_rows(r0):
        def write(val):
            o_ref[r0:r0 + tq, :] = val
        return write

    if latent:
        s = pl.program_id(1)

        @pl.when(s < nt)
        def _():
            project(0, pl.multiple_of(s * TM, TM), x_ref[...])

            @pl.when(s == 0)
            def _():
                first(0)

        @pl.when(s >= nt)
        def _():
            attend(0, pl.multiple_of((s - nt) * tq, tq), xq_ref[...], write_rows(0))
    else:
        for j in range(sps):
            first(j)
            project(j, 0, x_ref[j * n:(j + 1) * n, :])
        for j in range(sps):
            attend(j, 0, x_ref[j * n:(j + 1) * n, :], write_rows(j * n))


def _even_mixer(x, mod, l, n, latent, g_mix, w, tables, ctx, n_slots=1, prev_lat=None):
    slot = l // 2
    t = x.shape[0]
    nb = t // n
    nt = n // TM
    tq = min(n, TQ_MLA)
    ntq = n // tq
    nk = n + (PAST_LEN if latent else 0)
    sps = 1 if latent else CTX_SEQS_PER_STEP
    names = ["w_in", "g_q", "g_kv", "w_uq", "w_kcat", "w_uv", "w_pool", "pool_scale"]
    weights = [_layer_spec(g_mix, l), *[_layer_spec(w[k], slot) for k in names]]
    out_shape = [jax.ShapeDtypeStruct((t, D_MODEL), F32)]
    aliases = {}
    if latent:
        def qmap(b, s):
            return (b * ntq + jnp.maximum(s - nt, 0), 0)
        tspec = pl.BlockSpec((TM, LANES), lambda b, s: (jnp.minimum(s, nt - 1), 0))
        grid = (nb, nt + ntq)
        in_specs = [
            pl.BlockSpec((TM, D_MODEL), lambda b, s: (b * nt + jnp.minimum(s, nt - 1), 0)),
            pl.BlockSpec((tq, D_MODEL), qmap),
            pl.BlockSpec((1, 1, N_MOD, D_MODEL), lambda b, s: (l, 1 + b, 0, 0)),
            *weights,
            pl.BlockSpec((tq, POOL_DIM), lambda b, s: (jnp.maximum(s - nt, 0), 0)),
            _layer_spec(w["w_out"], slot),
            tspec, tspec, tspec,
            pl.BlockSpec((1, 1, PAST_LEN, 384), lambda b, s: (b, slot, 0, 0)),
        ]
        args = [x, x, mod, g_mix, *[w[k] for k in names], tables["cnt"], w["w_out"],
                tables["cos"], tables["sin_prev"], tables["sin_next"], ctx]
        out_specs = [pl.BlockSpec((tq, D_MODEL), qmap)]
    else:
        grid = (nb // sps,)
        in_specs = [
            pl.BlockSpec((sps * n, D_MODEL), lambda i: (i, 0)),
            pl.BlockSpec((1, 1, N_MOD, D_MODEL), lambda i: (l, 0, 0, 0)),
            *weights,
            _const_spec((tq, POOL_DIM)),
            _layer_spec(w["w_out"], slot),
        ]
        args = [x, mod, g_mix, *[w[k] for k in names], tables["cnt"], w["w_out"]]
        out_specs = [pl.BlockSpec((sps * n, D_MODEL), lambda i: (i, 0))]
        shape, spec = _state_specs(nb, n_slots, n, KV_LORA + QK_ROPE, slot, prev_lat, sps)
        out_shape.append(shape)
        out_specs.append(spec)
        if prev_lat is not None:
            aliases = {len(args): 1}
            in_specs.append(pl.BlockSpec(memory_space=pl.ANY))
            args.append(prev_lat)
    scratch = [
        pltpu.VMEM((MLA_HEADS, sps * n, LANES), BF16),
        pltpu.VMEM((MLA_HEADS, sps * nk, LANES), BF16),
        pltpu.VMEM((MLA_HEADS, sps * nk, LANES), BF16),
        pltpu.VMEM((sps * (n + 2 * POOL_PAD), POOL_DIM), F32),
        pltpu.VMEM((sps, MLA_HEADS // 2, tq, LANES), BF16),
    ]
    return pl.pallas_call(
        functools.partial(_even_body, n=n, tq=tq, latent=latent, slot=slot, sps=sps),
        out_shape=out_shape,
        grid=grid,
        in_specs=in_specs,
        out_specs=out_specs,
        scratch_shapes=scratch,
        input_output_aliases=aliases,
        compiler_params=_cparams(("arbitrary",) * len(grid)),
        name="even_latent" if latent else "even_context",
    )(*args)


def _bias_body(rpb_ref, o_ref):
    qc = lax.broadcasted_iota(jnp.int32, (GRID_W, LANES), 0)
    kc = lax.broadcasted_iota(jnp.int32, (GRID_W, LANES), 1)
    c0 = jnp.clip(qc - NA_KW // 2, 0, GRID_W - NA_KW)
    inwin = (kc >= c0) & (kc < c0 + NA_KW)
    blocks = []
    for dr in range(2 * NA_KH - 1):
        row = jnp.broadcast_to(rpb_ref[0, dr:dr + 1, :], (GRID_W, LANES))
        toep = pltpu.roll(row, LANES - (NA_KW - 1), 1, stride=1, stride_axis=0)
        blocks.append(jnp.where(inwin, toep, NEG)[:, :GRID_W])
    for v in range(NA_KH):
        for i in range(NA_KH):
            o_ref[0, v, :, GRID_W * i:GRID_W * (i + 1)] = blocks[v + i]


def _na_bias(rpb):
    nh, ndr, nj = rpb.shape
    rpb = jnp.pad(rpb, ((0, 0), (0, 0), (0, LANES - nj)))
    return pl.pallas_call(
        _bias_body,
        out_shape=jax.ShapeDtypeStruct((nh, NA_KH, GRID_W, NA_KH * GRID_W), F32),
        grid=(nh,),
        in_specs=[pl.BlockSpec((1, ndr, LANES), lambda h: (h, 0, 0))],
        out_specs=pl.BlockSpec((1, NA_KH, GRID_W, NA_KH * GRID_W), lambda h: (h, 0, 0, 0)),
        compiler_params=_cparams(("arbitrary",)),
        name="na_bias",
    )(rpb)


def _softmax_pv(parts, extra=None):
    mx = functools.reduce(jnp.maximum, [jnp.max(sc, axis=-1, keepdims=True) for sc, _ in parts])
    if extra is not None:
        mx = jnp.maximum(mx, extra)
    l = jnp.exp(extra - mx) if extra is not None else 0.0
    acc = None
    for sc, val in parts:
        e = jnp.exp(sc - mx)
        l = l + jnp.sum(e, axis=-1, keepdims=True)
        o = val(e.astype(BF16)) if callable(val) else _dot(e.astype(BF16), val)
        acc = o if acc is None else acc + o
    return acc / l


def _odd_body(*refs, n, latent, slot, sps):
    if latent:
        (x_ref, mod_ref, g_ref, win_ref, sink_ref, wout_ref, cos_ref, sp_ref, sn_ref, bias_ref,
         cna_ref, csw_ref,
         o_ref, qna, kna, vna, qsw, ksw, vsw, osc) = refs
        nakv_ref = swkv_ref = None
    else:
        x_ref, mod_ref, g_ref, win_ref, sink_ref, wout_ref = refs[:6]
        o_ref, nakv_ref, swkv_ref, qna, kna, vna, qsw, ksw, vsw, osc = refs[-10:]
    nt = n // TM
    nk = n + (PAST_LEN if latent else 0)
    nsw = n + (2 * SWA_WINDOW + PAST_LEN if latent else 0)
    m = mod_ref[0, 0]
    sw_off = SWA_WINDOW if latent else 0
    lane = lax.broadcasted_iota(jnp.int32, (TM, LANES), 1)
    lo = lane < HEAD_DIM

    def project(j, r0, x):
        h = (_rms(x, g_ref[0]) * (1.0 + m[1:2]) + m[0:1]).astype(BF16)
        z_sw = _dot(h, win_ref[0, :, 1536:2304])
        z = jnp.concatenate([_dot(h, win_ref[0, :, :1536]), z_sw], axis=1)
        if nakv_ref is not None:
            _write_state(nakv_ref, j, slot, z[:, 512:1536])
            _write_state(swkv_ref, j, slot, z[:, 2048:2304])
        scale = HEAD_DIM ** -0.5
        qrows, krows = pl.ds(j * n + r0, TM), pl.ds(j * nk + r0, TM)
        for p in range(NA_HEADS // 2):
            qp = z[:, LANES * p:LANES * (p + 1)] * scale
            qna[2 * p, qrows, :] = jnp.where(lo, qp, 0.0).astype(BF16)
            qna[2 * p + 1, qrows, :] = jnp.where(lo, 0.0, qp).astype(BF16)
            kna[p, krows, :] = z[:, 512 + LANES * p:512 + LANES * (p + 1)].astype(BF16)
            vna[p, krows, :] = z[:, 1024 + LANES * p:1024 + LANES * (p + 1)].astype(BF16)
        if latent:
            cos, sp, sn = cos_ref[...], sp_ref[...], sn_ref[...]
        for p in range(SWA_HEADS // 2):
            qp = z[:, 1536 + LANES * p:1536 + LANES * (p + 1)]
            if latent:
                qp = _rope(qp, cos, sp, sn, 16)
            qp = qp * scale
            qsw[2 * p, qrows, :] = jnp.where(lo, qp, 0.0).astype(BF16)
            qsw[2 * p + 1, qrows, :] = jnp.where(lo, 0.0, qp).astype(BF16)
        k = z[:, 2048:2176]
        if latent:
            k = _rope(k, cos, sp, sn, 16)
        v = z[:, 2176:2304]
        ksw_, vsw_ = pltpu.roll(k, HEAD_DIM, 1), pltpu.roll(v, HEAD_DIM, 1)
        swrows = pl.ds(j * nsw + sw_off + r0, TM)
        ksw[0, swrows, :] = jnp.where(lo, k, ksw_).astype(BF16)
        ksw[1, swrows, :] = jnp.where(lo, ksw_, k).astype(BF16)
        vsw[0, swrows, :] = jnp.where(lo, v, vsw_).astype(BF16)
        vsw[1, swrows, :] = jnp.where(lo, vsw_, v).astype(BF16)

    def first():
        cna = cna_ref[0, 0]
        for p in range(NA_HEADS // 2):
            kna[p, n:n + PAST_LEN, :] = cna[:, LANES * p:LANES * (p + 1)]
            vna[p, n:n + PAST_LEN, :] = cna[:, 512 + LANES * p:512 + LANES * (p + 1)]
        csw = csw_ref[0, 0]
        zpad = jnp.zeros((SWA_WINDOW, LANES), BF16)
        for kv in range(SWA_KV_HEADS):
            ksw[kv, 0:SWA_WINDOW, :] = zpad
            vsw[kv, 0:SWA_WINDOW, :] = zpad
            ksw[kv, n + SWA_WINDOW:n + 2 * SWA_WINDOW, :] = zpad
            vsw[kv, n + SWA_WINDOW:n + 2 * SWA_WINDOW, :] = zpad
            ksw[kv, n + 2 * SWA_WINDOW:n + 2 * SWA_WINDOW + PAST_LEN, :] = csw[:, LANES * kv:LANES * (kv + 1)]
            vsw[kv, n + 2 * SWA_WINDOW:n + 2 * SWA_WINDOW + PAST_LEN, :] = csw[:, 256 + LANES * kv:256 + LANES * (kv + 1)]

    def attend(j, ti, xq, write):
        q0 = ti * TM if isinstance(ti, int) else pl.multiple_of(ti * TM, TM)
        rows_per_tile = TM // GRID_W
        nrows = n // GRID_W
        nloc = NA_KH * GRID_W

        def na_pair(p):
            if latent:
                blk = 2 * GRID_W
                qst = jnp.concatenate(
                    [qna[2 * p + a, pl.ds(j * n + q0 + GRID_W * r, GRID_W), :]
                     for r in range(rows_per_tile) for a in (0, 1)], axis=0)
                s_loc, starts = [], []
                for r in range(rows_per_tile):
                    row = ti * rows_per_tile + r
                    rs = jnp.clip(row - NA_KH // 2, 0, nrows - NA_KH)
                    k0 = pl.multiple_of(rs * GRID_W, GRID_W)
                    starts.append(k0)
                    var = rs - row + NA_KH - 1
                    bias = jnp.concatenate([bias_ref[0, 2 * p, var], bias_ref[0, 2 * p + 1, var]], axis=0)
                    s_loc.append(_dot_t(qst[blk * r:blk * (r + 1)], kna[p, pl.ds(k0, nloc), :]) + bias)
                s_loc = jnp.concatenate(s_loc, axis=0)
                s_ctx = _dot_t(qst, kna[p, pl.ds(n, PAST_LEN), :])

                def pv_loc(e):
                    return jnp.concatenate(
                        [_dot(e[blk * r:blk * (r + 1)], vna[p, pl.ds(starts[r], nloc), :])
                         for r in range(rows_per_tile)], axis=0)

                o = _softmax_pv([(s_loc, pv_loc), (s_ctx, vna[p, pl.ds(n, PAST_LEN), :])])
                outs = [jnp.concatenate([o[blk * r + GRID_W * a:blk * r + GRID_W * (a + 1)]
                                         for r in range(rows_per_tile)], axis=0) for a in (0, 1)]
            else:
                qst = jnp.concatenate([qna[2 * p + a, pl.ds(j * n + q0, TM), :] for a in (0, 1)], axis=0)
                o = _softmax_pv([(_dot_t(qst, kna[p, j * nk:(j + 1) * nk, :]), vna[p, j * nk:(j + 1) * nk, :])])
                outs = [o[:TM], o[TM:]]
            osc[j, p] = jnp.where(lo, outs[0], outs[1]).astype(BF16)

        group = SWA_HEADS // SWA_KV_HEADS
        stack = SWA_STACK

        def swa_chunk(c):
            kv = c * stack // group
            qst = jnp.concatenate([qsw[stack * c + g, pl.ds(j * n + q0, TM), :] for g in range(stack)], axis=0)
            sink = jnp.concatenate(
                [jnp.full((TM, 1), sink_ref[slot, stack * c + g], F32) for g in range(stack)], axis=0)
            if latent:
                span = TM + 2 * SWA_WINDOW
                s_loc = _dot_t(qst, ksw[kv, pl.ds(q0, span), :])
                a_i = lax.broadcasted_iota(jnp.int32, (TM, span), 0)
                b_i = lax.broadcasted_iota(jnp.int32, (TM, span), 1)
                key = q0 - SWA_WINDOW + b_i
                ok = (b_i >= a_i) & (b_i <= a_i + 2 * SWA_WINDOW) & (key >= 0) & (key < n)
                s_loc = jnp.concatenate(
                    [jnp.where(ok, s_loc[TM * g:TM * (g + 1)], NEG) for g in range(stack)], axis=0)
                c0 = n + 2 * SWA_WINDOW
                s_ctx = _dot_t(qst, ksw[kv, pl.ds(c0, PAST_LEN), :])
                o = _softmax_pv([(s_loc, vsw[kv, pl.ds(q0, span), :]),
                                 (s_ctx, vsw[kv, pl.ds(c0, PAST_LEN), :])], extra=sink)
            else:
                o = _softmax_pv([(_dot_t(qst, ksw[kv, j * nsw:(j + 1) * nsw, :]),
                                  vsw[kv, j * nsw:(j + 1) * nsw, :])], extra=sink)
            for u in range(stack // 2):
                osc[j, NA_HEADS // 2 + (stack // 2) * c + u] = jnp.where(
                    lo, o[TM * 2 * u:TM * (2 * u + 1)], o[TM * (2 * u + 1):TM * (2 * u + 2)]).astype(BF16)

        if latent:
            def na_step(i, carry):
                na_pair(2 * i)
                na_pair(2 * i + 1)
                return carry
            lax.fori_loop(0, NA_HEADS // 4, na_step, 0)
        else:
            for p in range(NA_HEADS // 2):
                na_pair(p)
        for c in range(SWA_HEADS // stack):
            swa_chunk(c)

        mix = jnp.concatenate([osc[j, p] for p in range(8)], axis=1)
        write(xq + m[2:3] * _dot(mix, wout_ref[0]))

    if latent:
        s = pl.program_id(1)

        @pl.when(s < nt)
        def _():
            project(0, pl.multiple_of(s * TM, TM), x_ref[...])

            @pl.when(s == 0)
            def _():
                first()

        @pl.when(s >= nt)
        def _():
            def write(val):
                o_ref[...] = val
            attend(0, s - nt, x_ref[...], write)
    else:
        for j in range(sps):
            project(j, 0, x_ref[j * n:(j + 1) * n, :])

        @pl.when(pl.program_id(0) >= 0)
        def _():
            for j in range(sps):
                def write(val, j=j):
                    o_ref[j * n:(j + 1) * n, :] = val
                attend(j, 0, x_ref[j * n:(j + 1) * n, :], write)


def _odd_mixer(x, mod, l, n, latent, g_mix, w, tables, bias, cna, csw, n_slots=1, prev=None):
    slot = l // 2
    t = x.shape[0]
    nb = t // n
    nt = n // TM
    nk = n + (PAST_LEN if latent else 0)
    nsw = n + (2 * SWA_WINDOW + PAST_LEN if latent else 0)
    sps = 1 if latent else CTX_SEQS_PER_STEP
    weights = [_layer_spec(g_mix, l), _layer_spec(w["w_in"], slot), pl.BlockSpec(memory_space=pltpu.SMEM),
               _layer_spec(w["w_out"], slot)]
    args = [x, mod, g_mix, w["w_in"], w["sink"], w["w_out"]]
    out_shape = [jax.ShapeDtypeStruct((t, D_MODEL), F32)]
    aliases = {}
    if latent:
        grid = (nb, 2 * nt)
        tspec = pl.BlockSpec((TM, LANES), lambda b, s: (jnp.minimum(s, nt - 1), 0))
        in_specs = [pl.BlockSpec((TM, D_MODEL), lambda b, s: (b * nt + s % nt, 0)),
                    pl.BlockSpec((1, 1, N_MOD, D_MODEL), lambda b, s: (l, 1 + b, 0, 0)),
                    *weights, tspec, tspec, tspec, _layer_spec(bias, slot),
                    pl.BlockSpec((1, 1, PAST_LEN, 1024), lambda b, s: (b, slot, 0, 0)),
                    pl.BlockSpec((1, 1, PAST_LEN, 512), lambda b, s: (b, slot, 0, 0))]
        args += [tables["cos"], tables["sin_prev"], tables["sin_next"], bias, cna, csw]
        out_specs = [pl.BlockSpec((TM, D_MODEL), lambda b, s: (b * nt + jnp.maximum(s - nt, 0), 0))]
    else:
        grid = (nb // sps,)
        in_specs = [pl.BlockSpec((sps * n, D_MODEL), lambda i: (i, 0)),
                    pl.BlockSpec((1, 1, N_MOD, D_MODEL), lambda i: (l, 0, 0, 0)), *weights]
        out_specs = [pl.BlockSpec((sps * n, D_MODEL), lambda i: (i, 0))]
        for k, width in enumerate((2 * NA_HEADS * HEAD_DIM, 2 * SWA_KV_HEADS * HEAD_DIM)):
            shape, spec = _state_specs(nb, n_slots, n, width, slot, prev, sps)
            out_shape.append(shape)
            out_specs.append(spec)
            if prev is not None:
                aliases[len(args)] = 1 + k
                in_specs.append(pl.BlockSpec(memory_space=pl.ANY))
                args.append(prev[k])
    scratch = [
        pltpu.VMEM((NA_HEADS, sps * n, LANES), BF16),
        pltpu.VMEM((NA_HEADS // 2, sps * nk, LANES), BF16),
        pltpu.VMEM((NA_HEADS // 2, sps * nk, LANES), BF16),
        pltpu.VMEM((SWA_HEADS, sps * n, LANES), BF16),
        pltpu.VMEM((SWA_KV_HEADS, sps * nsw, LANES), BF16),
        pltpu.VMEM((SWA_KV_HEADS, sps * nsw, LANES), BF16),
        pltpu.VMEM((sps, 8, TM, LANES), BF16),
    ]
    return pl.pallas_call(
        functools.partial(_odd_body, n=n, latent=latent, slot=slot, sps=sps),
        out_shape=out_shape,
        grid=grid,
        in_specs=in_specs,
        out_specs=out_specs,
        scratch_shapes=scratch,
        input_output_aliases=aliases,
        compiler_params=_cparams(("arbitrary",) * len(grid)),
        name="odd_latent" if latent else "odd_context",
    )(*args)


def _rope_tables(n, rot, lane0, period):
    half = rot // 2
    nf = half // 2
    t = np.arange(n)
    freqs = (ROPE_BASE ** (-np.arange(nf, dtype=np.float32) / nf)).astype(np.float32)
    cos = np.ones((n, period), np.float32)
    sp = np.zeros((n, period), np.float32)
    sn = np.zeros((n, period), np.float32)
    for k, pos in enumerate((t // GRID_W, t % GRID_W)):
        ang = pos.astype(np.float32)[:, None] * freqs
        c, s_ = np.cos(ang), np.sin(ang)
        a = lane0 + k * half
        cos[:, a:a + nf] = c
        cos[:, a + nf:a + half] = c
        sn[:, a:a + nf] = -s_
        sp[:, a + nf:a + half] = s_
    reps = LANES // period
    return {"cos": jnp.asarray(np.tile(cos, (1, reps))), "sin_prev": jnp.asarray(np.tile(sp, (1, reps))),
            "sin_next": jnp.asarray(np.tile(sn, (1, reps)))}


def _pool_counts(n):
    t = np.arange(n)
    cols = []
    for wdw in POOL_WINDOWS:
        lo = np.clip(t - wdw // 2, 0, n)
        hi = np.clip(t + wdw // 2, 0, n)
        cols.append(np.repeat((hi - lo).astype(np.float32)[:, None], POOL_GROUP, axis=1))
    return jnp.asarray(np.concatenate(cols, axis=1))


def _even_weights(w_in, g_q, g_kv, w_uq, w_ukv, w_pool, pool_scale, w_out):
    ne = w_in.shape[0]
    b_, c_ = Q_LORA + KV_LORA, Q_LORA + KV_LORA + QK_ROPE
    krpad = jnp.pad(w_in[..., b_:c_], ((0, 0), (0, 0), (64, LANES - 64 - QK_ROPE)))
    w_in_p = jnp.concatenate([w_in[..., :b_], w_in[..., c_:], krpad], axis=-1)
    uq = w_uq.reshape(ne, Q_LORA, MLA_HEADS, QK_NOPE + QK_ROPE)
    uq = jnp.pad(uq, ((0, 0),) * 3 + ((0, LANES - QK_NOPE - QK_ROPE),)).reshape(ne, Q_LORA, MLA_HEADS * LANES)
    ukv = w_ukv.reshape(ne, KV_LORA, MLA_HEADS, QK_NOPE + V_DIM)
    uk = jnp.pad(ukv[..., :QK_NOPE], ((0, 0),) * 3 + ((0, LANES - QK_NOPE),))
    sel = np.zeros((LANES, MLA_HEADS, LANES), np.float32)
    sel[64 + np.arange(QK_ROPE), :, 64 + np.arange(QK_ROPE)] = 1.0
    sel = jnp.broadcast_to(jnp.asarray(sel), (ne,) + sel.shape)
    w_kcat = jnp.concatenate([uk, sel], axis=1).reshape(ne, KV_LORA + LANES, MLA_HEADS * LANES)
    w_uv = ukv[..., QK_NOPE:].reshape(ne, KV_LORA, MLA_HEADS * V_DIM)
    eye = jnp.asarray(np.eye(len(POOL_WINDOWS), dtype=np.float32))
    w_pool_bd = (w_pool[:, :, :, None, :] * eye[None, :, None, :, None]).reshape(ne, POOL_DIM, POOL_DIM)
    return {"w_in": w_in_p.astype(BF16), "g_q": g_q.reshape(ne, 1, -1), "g_kv": g_kv.reshape(ne, 1, -1),
            "w_uq": uq.astype(BF16), "w_kcat": w_kcat.astype(BF16), "w_uv": w_uv.astype(BF16),
            "w_pool": w_pool_bd.astype(BF16), "pool_scale": pool_scale.reshape(ne, 1, -1),
            "w_out": w_out.astype(BF16)}


def kernel(x_prompt, x_sample, cache_mla_latent, cache_na_kv, cache_swa_kv, c, c_ctx, w_mod, b_mod, norm_mix, norm_ffn, norm_final, w_in_even, mla_q_norm, mla_kv_norm, w_uq, w_ukv, w_pool, pool_scale, w_out_even, w_in_odd, na_rpb, swa_sink, w_out_odd, w_up, conv_w, conv_b, w_down):
    nbp, n_p, _ = x_prompt.shape
    nbs, n_s, _ = x_sample.shape
    depth = w_mod.shape[0]
    xp = x_prompt.reshape(nbp * n_p, D_MODEL)
    xs = x_sample.reshape(nbs * n_s, D_MODEL)

    c_all = jnp.zeros((8, D_MODEL), F32).at[0].set(c_ctx).at[1:1 + nbs].set(c)
    mod = _adaln(c_all, w_mod, b_mod).reshape(depth, 8, N_MOD, D_MODEL)

    tab_mla = _rope_tables(n_s, QK_ROPE, 64, LANES)
    tab_swa = _rope_tables(n_s, HEAD_DIM, 0, HEAD_DIM)
    cnt_p, cnt_s = _pool_counts(n_p), _pool_counts(n_s)

    bias_all = _na_bias(na_rpb.reshape((-1,) + na_rpb.shape[2:])).reshape(
        na_rpb.shape[:2] + (NA_KH, GRID_W, NA_KH * GRID_W))

    n_even, n_odd = (depth + 1) // 2, depth // 2
    g_mix = norm_mix.reshape(depth, 1, D_MODEL)
    w_even = _even_weights(w_in_even, mla_q_norm, mla_kv_norm, w_uq, w_ukv, w_pool, pool_scale, w_out_even)
    w_odd = {"w_in": w_in_odd.astype(BF16), "w_out": w_out_odd.astype(BF16), "sink": swa_sink}
    w_ffn = {"g_ffn": norm_ffn.reshape(depth, 1, D_MODEL), "w_up": w_up.astype(BF16), "conv_w": conv_w,
             "conv_b": conv_b.reshape(depth, 1, -1), "w_down": w_down.astype(BF16),
             "g_final": norm_final.reshape(1, -1)}
    zeros = functools.partial(jnp.zeros, dtype=F32)
    lead = cache_mla_latent.shape[:3]
    ctx = jnp.concatenate([cache_mla_latent[..., :KV_LORA], zeros(lead + (64,)),
                           cache_mla_latent[..., KV_LORA:], zeros(lead + (32,))], axis=-1).astype(BF16)
    cna = cache_na_kv.reshape(nbs, -1, PAST_LEN, 2 * NA_HEADS * HEAD_DIM).astype(BF16)
    csw = cache_swa_kv.reshape(nbs, -1, PAST_LEN, 2 * SWA_KV_HEADS, HEAD_DIM)
    csw = jnp.concatenate([csw, csw], axis=-1).reshape(nbs, -1, PAST_LEN, 512).astype(BF16)

    lat = kv = None
    for l in range(depth):
        if l % 2 == 0:
            xp, lat = _even_mixer(xp, mod, l, n_p, False, g_mix, w_even, {"cnt": cnt_p}, None,
                                  n_slots=n_even, prev_lat=lat)
            (xs,) = _even_mixer(xs, mod, l, n_s, True, g_mix, w_even, dict(tab_mla, cnt=cnt_s), ctx)
        else:
            xp, *kv = _odd_mixer(xp, mod, l, n_p, False, g_mix, w_odd, None, None, None, None,
                                 n_slots=n_odd, prev=kv)
            (xs,) = _odd_mixer(xs, mod, l, n_s, True, g_mix, w_odd, tab_swa, bias_all, cna, csw)
        xp = _ffn(xp, n_p, False, mod, l, w_ffn, l == depth - 1)
        xs = _ffn(xs, n_s, True, mod, l, w_ffn, l == depth - 1)

    return (xp.reshape(nbp, n_p, D_MODEL), xs.reshape(nbs, n_s, D_MODEL), lat,
            kv[0].reshape(nbp, n_odd, n_p, 2, NA_HEADS, HEAD_DIM),
            kv[1].reshape(nbp, n_odd, n_p, 2, SWA_KV_HEADS, HEAD_DIM))
```

```python
import functools

import numpy as np
import jax
import jax.numpy as jnp
from jax import lax
from jax.experimental import pallas as pl
from jax.experimental.pallas import tpu as pltpu

F32 = jnp.float32
BF16 = jnp.bfloat16

D_MODEL = 1024
GRID_W = 64
N_MOD = 6
EPS = 1e-6
ROPE_BASE = 10000.0
NEG = -1e30
MLA_HEADS = 12
Q_LORA = 384
KV_LORA = 256
QK_NOPE = 64
QK_ROPE = 32
V_DIM = 64
MLA_SCALE = (QK_NOPE + QK_ROPE) ** -0.5
LOG2E = 1.4426950408889634
POOL_WINDOWS = (2, 4, 8, 16)
POOL_GROUP = 64
POOL_DIM = 256
NA_HEADS = 8
NA_KH = 8
NA_KW = 16
SWA_HEADS = 8
SWA_KV_HEADS = 2
SWA_WINDOW = 128
HEAD_DIM = 64
D_FF = 2816
PAST_LEN = 512

LANES = 128
TM = 256
TQ_MLA = 512
SWA_STACK = 2
CTX_SEQS_PER_STEP = 1
FF_CHUNK = 256
FFN_TILES = 2
N_FF_CHUNKS = D_FF // FF_CHUNK
POOL_PAD = 16
VMEM_LIMIT = 60000 * 1024


def _cparams(sem):
    return pltpu.CompilerParams(dimension_semantics=sem, vmem_limit_bytes=VMEM_LIMIT)


def _rms(x, g):
    return x * lax.rsqrt(jnp.mean(x * x, axis=-1, keepdims=True) + EPS) * g


def _silu(x):
    return x * (1.0 / (1.0 + jnp.exp(-x)))


def _dot(a, b):
    return jnp.dot(a, b, preferred_element_type=F32)


def _dot_t(a, b):
    return lax.dot_general(a, b, (((1,), (1,)), ((), ())), preferred_element_type=F32)


def _rope(x, cos, sin_prev, sin_next, shift):
    w = x.shape[-1]
    return x * cos + pltpu.roll(x, shift, 1) * sin_prev + pltpu.roll(x, w - shift, 1) * sin_next


def _const_spec(shape):
    nd = len(shape)
    return pl.BlockSpec(shape, lambda *_: (0,) * nd, pipeline_mode=pl.Buffered(1))


def _layer_spec(arr, l):
    shape = (1,) + arr.shape[1:]
    tail = (0,) * (arr.ndim - 1)
    return pl.BlockSpec(shape, lambda *_: (l,) + tail, pipeline_mode=pl.Buffered(1))


def _adaln_body(c_ref, w_ref, b_ref, o_ref):
    a = _silu(c_ref[...]).astype(BF16)
    o_ref[0] = _dot(a, w_ref[0].astype(BF16)) + b_ref[0]


def _adaln(c_all, w_mod, b_mod):
    depth, _, width = w_mod.shape
    tn = 1536
    return pl.pallas_call(
        _adaln_body,
        out_shape=jax.ShapeDtypeStruct((depth, 8, width), F32),
        grid=(depth, width // tn),
        in_specs=[
            pl.BlockSpec((8, D_MODEL), lambda l, j: (0, 0)),
            pl.BlockSpec((1, D_MODEL, tn), lambda l, j: (l, 0, j)),
            pl.BlockSpec((1, 1, tn), lambda l, j: (l, 0, j)),
        ],
        out_specs=pl.BlockSpec((1, 8, tn), lambda l, j: (l, 0, j)),
        compiler_params=_cparams(("arbitrary", "arbitrary")),
        name="adaln",
    )(c_all, w_mod, b_mod.reshape(depth, 1, width))


def _ffn_body(x_ref, x_prev_ref, x_next_ref, mod_ref, g_ref, wup_ref, cw_ref, cb_ref, wdn_ref, gf_ref,
              o_ref, act_ref, *, n, final):
    i = pl.program_id(0)
    m = mod_ref[0, 0]
    shift, scale, gate = m[3:4], m[4:5], m[5:6]
    g = g_ref[0]

    def hn(x):
        return _rms(x, g) * (1.0 + scale) + shift

    for k in range(FFN_TILES):
        x = x_ref[k * TM:(k + 1) * TM, :]
        prev = x_prev_ref[...] if k == 0 else x_ref[k * TM - 8:k * TM, :]
        nxt = x_next_ref[...] if k == FFN_TILES - 1 else x_ref[(k + 1) * TM:(k + 1) * TM + 8, :]
        tile_in_seq = (i * FFN_TILES + k) % (n // TM)
        keep_prev = jnp.where(tile_in_seq == 0, 0.0, 1.0)
        keep_next = jnp.where(tile_in_seq == n // TM - 1, 0.0, 1.0)
        hext = jnp.concatenate([hn(prev) * keep_prev, hn(x), hn(nxt) * keep_next], axis=0).astype(BF16)

        def conv(c0, hext=hext):
            u = _dot(hext, wup_ref[0, :, c0:c0 + FF_CHUNK])
            w = cw_ref[0, :, c0:c0 + FF_CHUNK]
            return (pltpu.roll(u, 1, 0)[8:8 + TM] * w[0:1] + u[8:8 + TM] * w[1:2]
                    + pltpu.roll(u, TM + 15, 0)[8:8 + TM] * w[2:3] + cb_ref[0, :, c0:c0 + FF_CHUNK])

        for j in range(N_FF_CHUNKS):
            act_ref[k, :, j * FF_CHUNK:(j + 1) * FF_CHUNK] = (
                _silu(conv(D_FF + j * FF_CHUNK)) * conv(j * FF_CHUNK)).astype(BF16)
        y = x + gate * _dot(act_ref[k], wdn_ref[0])
        if final:
            y = _rms(y, gf_ref[...])
        o_ref[k * TM:(k + 1) * TM, :] = y


def _ffn(x, n, latent, mod, l, w, final):
    t = x.shape[0]
    nblk8 = t // 8
    rows = FFN_TILES * TM

    def gmap(i):
        return (l, 1 + i // (n // rows) if latent else 0, 0, 0)

    weights = [w["g_ffn"], w["w_up"], w["conv_w"], w["conv_b"], w["w_down"]]
    return pl.pallas_call(
        functools.partial(_ffn_body, n=n, final=final),
        out_shape=jax.ShapeDtypeStruct((t, D_MODEL), F32),
        grid=(t // rows,),
        in_specs=[
            pl.BlockSpec((rows, D_MODEL), lambda i: (i, 0)),
            pl.BlockSpec((8, D_MODEL), lambda i: (jnp.maximum(i * (rows // 8) - 1, 0), 0)),
            pl.BlockSpec((8, D_MODEL), lambda i: (jnp.minimum((i + 1) * (rows // 8), nblk8 - 1), 0)),
            pl.BlockSpec((1, 1, N_MOD, D_MODEL), gmap),
            *[_layer_spec(a, l) for a in weights],
            _const_spec((1, D_MODEL)),
        ],
        out_specs=pl.BlockSpec((rows, D_MODEL), lambda i: (i, 0)),
        scratch_shapes=[pltpu.VMEM((FFN_TILES, TM, D_FF), BF16)],
        compiler_params=_cparams(("arbitrary",)),
        name="conv_ffn",
    )(x, x, x, mod, *weights, w["g_final"])


def _state_specs(nb, n_slots, n, width, slot, prev, sps):
    shape = jax.ShapeDtypeStruct((nb, n_slots, n, width), F32)
    if prev is None:
        return shape, pl.BlockSpec((sps, n_slots, n, width), lambda i: (i, 0, 0, 0))
    return shape, pl.BlockSpec((sps, 1, n, width), lambda i: (i, slot, 0, 0))


def _write_state(ref, j, slot, val):
    if ref.shape[1] == 1:
        ref[j, 0] = val
    else:
        for k in range(ref.shape[1]):
            ref[j, k] = val if k == slot else jnp.zeros_like(val)


def _even_body(*refs, n, tq, latent, slot, sps):
    if latent:
        (x_ref, xq_ref, mod_ref, g_ref, win_ref, gq_ref, gkv_ref, wuq_ref, wkc_ref, wuv_ref, wpool_ref,
         pscale_ref, cnt_ref, wout_ref, cos_ref, sp_ref, sn_ref, ctx_ref,
         o_ref, qs, ks, vs, xps, osc) = refs
        lat_ref = None
    else:
        (x_ref, mod_ref, g_ref, win_ref, gq_ref, gkv_ref, wuq_ref, wkc_ref, wuv_ref, wpool_ref,
         pscale_ref, cnt_ref, wout_ref) = refs[:13]
        o_ref, lat_ref, qs, ks, vs, xps, osc = refs[-7:]
    nt = n // TM
    nk = n + (PAST_LEN if latent else 0)
    npool = n + 2 * POOL_PAD
    m = mod_ref[0, 0]
    lane = lax.broadcasted_iota(jnp.int32, (TM, LANES), 1)
    lo = lax.broadcasted_iota(jnp.int32, (tq, LANES), 1) < V_DIM

    def store_v(v, rows):
        for p in range(MLA_HEADS // 2):
            vp = v[:, LANES * p:LANES * (p + 1)]
            ln = lane[:v.shape[0]]
            vs[2 * p, rows, :] = jnp.where(ln < V_DIM, vp, jnp.where(ln == V_DIM, 1.0, 0.0)).astype(BF16)
            vs[2 * p + 1, rows, :] = jnp.where(ln >= V_DIM, vp, jnp.where(ln == 0, 1.0, 0.0)).astype(BF16)

    def project(j, r0, x):
        h = (_rms(x, g_ref[0]) * (1.0 + m[1:2]) + m[0:1]).astype(BF16)
        z = _dot(h, win_ref[0])
        qn = _rms(z[:, :Q_LORA], gq_ref[0]).astype(BF16)
        q = _dot(qn, wuq_ref[0])
        latc = _rms(z[:, Q_LORA:Q_LORA + KV_LORA], gkv_ref[0])
        krp = z[:, 896:1024]
        if latent:
            cos, sp, sn = cos_ref[...], sp_ref[...], sn_ref[...]
            krp = _rope(krp, cos, sp, sn, 8)
        if lat_ref is not None:
            kr0 = pltpu.roll(krp, LANES - 64, 1)
            _write_state(lat_ref, j, slot, jnp.concatenate([latc, kr0[:, :QK_ROPE]], axis=1))
        lb = jnp.concatenate([latc, krp], axis=1).astype(BF16)
        kc = _dot(lb, wkc_ref[0])
        v = _dot(lb[:, :KV_LORA], wuv_ref[0])
        for hd in range(MLA_HEADS):
            qh = q[:, LANES * hd:LANES * (hd + 1)]
            if latent:
                qh = _rope(qh, cos, sp, sn, 8)
            qs[hd, pl.ds(j * n + r0, TM), :] = (qh * (MLA_SCALE * LOG2E)).astype(BF16)
            ks[hd, pl.ds(j * nk + r0, TM), :] = kc[:, LANES * hd:LANES * (hd + 1)].astype(BF16)
        store_v(v, pl.ds(j * nk + r0, TM))
        xps[pl.ds(j * npool + POOL_PAD + r0, TM), :] = z[:, 640:896]

    def first(j):
        xps[j * npool:j * npool + POOL_PAD, :] = jnp.zeros((POOL_PAD, POOL_DIM), F32)
        xps[j * npool + n + POOL_PAD:(j + 1) * npool, :] = jnp.zeros((POOL_PAD, POOL_DIM), F32)
        if latent:
            cx = ctx_ref[0, 0]
            rows = cx.shape[0]
            cb = jnp.concatenate([cx[:, :KV_LORA], jnp.zeros((rows, 64), F32), cx[:, KV_LORA:],
                                  jnp.zeros((rows, LANES - 64 - QK_ROPE), F32)], axis=1).astype(BF16)
            kcc = _dot(cb, wkc_ref[0])
            for hd in range(MLA_HEADS):
                ks[hd, n:n + PAST_LEN, :] = kcc[:, LANES * hd:LANES * (hd + 1)].astype(BF16)
            for half in range(PAST_LEN // TM):
                vc = _dot(cb[TM * half:TM * (half + 1), :KV_LORA], wuv_ref[0])
                store_v(vc, pl.ds(n + TM * half, TM))

    def attend(j, q0, xq, write):
        def head(hd, l_lane):
            sc = _dot_t(qs[hd, pl.ds(j * n + q0, tq), :], ks[hd, j * nk:(j + 1) * nk, :])
            e = jnp.exp2(sc - jnp.max(sc, axis=-1, keepdims=True))
            o = _dot(e.astype(BF16), vs[hd, j * nk:(j + 1) * nk, :])
            return o * (1.0 / o[:, l_lane:l_lane + 1])

        for p in range(MLA_HEADS // 2):
            osc[j, p] = jnp.where(lo, head(2 * p, V_DIM), head(2 * p + 1, 0)).astype(BF16)

        rows = tq + 2 * POOL_PAD
        a0 = xps[pl.ds(j * npool + q0, rows), :]
        s1 = a0 + pltpu.roll(a0, 1, 0)
        s2 = s1 + pltpu.roll(s1, 2, 0)
        s4 = s2 + pltpu.roll(s2, 4, 0)
        s8 = s4 + pltpu.roll(s4, 8, 0)
        pl_lane = lax.broadcasted_iota(jnp.int32, (rows, POOL_DIM), 1)
        win = jnp.where(pl_lane < 64, s1,
                        jnp.where(pl_lane < 128, pltpu.roll(s2, rows - 1, 0),
                                  jnp.where(pl_lane < 192, pltpu.roll(s4, rows - 3, 0),
                                            pltpu.roll(s8, rows - 7, 0))))
        pooled = win[POOL_PAD:POOL_PAD + tq] / cnt_ref[...] - a0[POOL_PAD:POOL_PAD + tq]
        ypool = _dot(pooled.astype(BF16), wpool_ref[0]) * pscale_ref[0]

        mix = jnp.concatenate([osc[j, p] for p in range(MLA_HEADS // 2)] + [ypool.astype(BF16)], axis=1)
        write(xq + m[2:3] * _dot(mix, wout_ref[0]))

    def write_rows(r0):
        def write(val):
            o_ref[r0:r0 + tq, :] = val
        return write

    if latent:
        s = pl.program_id(1)

        @pl.when(s < nt)
        def _():
            project(0, pl.multiple_of(s * TM, TM), x_ref[...])

            @pl.when(s == 0)
            def _():
                first(0)

        @pl.when(s >= nt)
        def _():
            attend(0, pl.multiple_of((s - nt) * tq, tq), xq_ref[...], write_rows(0))
    else:
        for j in range(sps):
            first(j)
            project(j, 0, x_ref[j * n:(j + 1) * n, :])
        for j in range(sps):
            attend(j, 0, x_ref[j * n:(j + 1) * n, :], write_rows(j * n))


def _even_mixer(x, mod, l, n, latent, g_mix, w, tables, ctx, n_slots=1, prev_lat=None):
    slot = l // 2
    t = x.shape[0]
    nb = t // n
    nt = n // TM
    tq = min(n, TQ_MLA)
    ntq = n // tq
    nk = n + (PAST_LEN if latent else 0)
    sps = 1 if latent else CTX_SEQS_PER_STEP
    names = ["w_in", "g_q", "g_kv", "w_uq", "w_kcat", "w_uv", "w_pool", "pool_scale"]
    weights = [_layer_spec(g_mix, l), *[_layer_spec(w[k], slot) for k in names]]
    out_shape = [jax.ShapeDtypeStruct((t, D_MODEL), F32)]
    aliases = {}
    if latent:
        def qmap(b, s):
            return (b * ntq + jnp.maximum(s - nt, 0), 0)
        tspec = pl.BlockSpec((TM, LANES), lambda b, s: (jnp.minimum(s, nt - 1), 0))
        grid = (nb, nt + ntq)
        in_specs = [
            pl.BlockSpec((TM, D_MODEL), lambda b, s: (b * nt + jnp.minimum(s, nt - 1), 0)),
            pl.BlockSpec((tq, D_MODEL), qmap),
            pl.BlockSpec((1, 1, N_MOD, D_MODEL), lambda b, s: (l, 1 + b, 0, 0)),
            *weights,
            pl.BlockSpec((tq, POOL_DIM), lambda b, s: (jnp.maximum(s - nt, 0), 0)),
            _layer_spec(w["w_out"], slot),
            tspec, tspec, tspec,
            pl.BlockSpec((1, 1, PAST_LEN, KV_LORA + QK_ROPE), lambda b, s: (b, slot, 0, 0)),
        ]
        args = [x, x, mod, g_mix, *[w[k] for k in names], tables["cnt"], w["w_out"],
                tables["cos"], tables["sin_prev"], tables["sin_next"], ctx]
        out_specs = [pl.BlockSpec((tq, D_MODEL), qmap)]
    else:
        grid = (nb // sps,)
        in_specs = [
            pl.BlockSpec((sps * n, D_MODEL), lambda i: (i, 0)),
            pl.BlockSpec((1, 1, N_MOD, D_MODEL), lambda i: (l, 0, 0, 0)),
            *weights,
            _const_spec((tq, POOL_DIM)),
            _layer_spec(w["w_out"], slot),
        ]
        args = [x, mod, g_mix, *[w[k] for k in names], tables["cnt"], w["w_out"]]
        out_specs = [pl.BlockSpec((sps * n, D_MODEL), lambda i: (i, 0))]
        shape, spec = _state_specs(nb, n_slots, n, KV_LORA + QK_ROPE, slot, prev_lat, sps)
        out_shape.append(shape)
        out_specs.append(spec)
        if prev_lat is not None:
            aliases = {len(args): 1}
            in_specs.append(pl.BlockSpec(memory_space=pl.ANY))
            args.append(prev_lat)
    scratch = [
        pltpu.VMEM((MLA_HEADS, sps * n, LANES), BF16),
        pltpu.VMEM((MLA_HEADS, sps * nk, LANES), BF16),
        pltpu.VMEM((MLA_HEADS, sps * nk, LANES), BF16),
        pltpu.VMEM((sps * (n + 2 * POOL_PAD), POOL_DIM), F32),
        pltpu.VMEM((sps, MLA_HEADS // 2, tq, LANES), BF16),
    ]
    return pl.pallas_call(
        functools.partial(_even_body, n=n, tq=tq, latent=latent, slot=slot, sps=sps),
        out_shape=out_shape,
        grid=grid,
        in_specs=in_specs,
        out_specs=out_specs,
        scratch_shapes=scratch,
        input_output_aliases=aliases,
        compiler_params=_cparams(("arbitrary",) * len(grid)),
        name="even_latent" if latent else "even_context",
    )(*args)


def _bias_body(rpb_ref, o_ref):
    qc = lax.broadcasted_iota(jnp.int32, (GRID_W, LANES), 0)
    kc = lax.broadcasted_iota(jnp.int32, (GRID_W, LANES), 1)
    c0 = jnp.clip(qc - NA_KW // 2, 0, GRID_W - NA_KW)
    inwin = (kc >= c0) & (kc < c0 + NA_KW)
    blocks = []
    for dr in range(2 * NA_KH - 1):
        row = jnp.broadcast_to(rpb_ref[0, dr:dr + 1, :], (GRID_W, LANES))
        toep = pltpu.roll(row, LANES - (NA_KW - 1), 1, stride=1, stride_axis=0)
        blocks.append(jnp.where(inwin, toep * LOG2E, NEG)[:, :GRID_W])
    for v in range(NA_KH):
        for i in range(NA_KH):
            o_ref[0, v, :, GRID_W * i:GRID_W * (i + 1)] = blocks[v + i]


def _na_bias(rpb):
    nh, ndr, nj = rpb.shape
    rpb = jnp.pad(rpb, ((0, 0), (0, 0), (0, LANES - nj)))
    return pl.pallas_call(
        _bias_body,
        out_shape=jax.ShapeDtypeStruct((nh, NA_KH, GRID_W, NA_KH * GRID_W), F32),
        grid=(nh,),
        in_specs=[pl.BlockSpec((1, ndr, LANES), lambda h: (h, 0, 0))],
        out_specs=pl.BlockSpec((1, NA_KH, GRID_W, NA_KH * GRID_W), lambda h: (h, 0, 0, 0)),
        compiler_params=_cparams(("arbitrary",)),
        name="na_bias",
    )(rpb)


def _softmax_pv(parts, extra=None):
    mx = functools.reduce(jnp.maximum, [jnp.max(sc, axis=-1, keepdims=True) for sc, _ in parts])
    if extra is not None:
        mx = jnp.maximum(mx, extra)
    l = jnp.exp2(extra - mx) if extra is not None else 0.0
    acc = None
    for sc, val in parts:
        e = jnp.exp2(sc - mx)
        l = l + jnp.sum(e, axis=-1, keepdims=True)
        o = val(e.astype(BF16)) if callable(val) else _dot(e.astype(BF16), val)
        acc = o if acc is None else acc + o
    return acc / l


def _odd_body(*refs, n, latent, slot, sps):
    if latent:
        (x_ref, mod_ref, g_ref, win_ref, sink_ref, wout_ref, cos_ref, sp_ref, sn_ref, bias_ref,
         cna_ref, csw_ref,
         o_ref, qna, kna, vna, qsw, ksw, vsw, osc) = refs
        nakv_ref = swkv_ref = None
    else:
        x_ref, mod_ref, g_ref, win_ref, sink_ref, wout_ref = refs[:6]
        o_ref, nakv_ref, swkv_ref, qna, kna, vna, qsw, ksw, vsw, osc = refs[-10:]
    nt = n // TM
    nk = n + (PAST_LEN if latent else 0)
    nsw = n + (2 * SWA_WINDOW + PAST_LEN if latent else 0)
    m = mod_ref[0, 0]
    sw_off = SWA_WINDOW if latent else 0
    lane = lax.broadcasted_iota(jnp.int32, (TM, LANES), 1)
    lo = lane < HEAD_DIM

    def project(j, r0, x):
        h = (_rms(x, g_ref[0]) * (1.0 + m[1:2]) + m[0:1]).astype(BF16)
        z_sw = _dot(h, win_ref[0, :, 1536:2304])
        z = jnp.concatenate([_dot(h, win_ref[0, :, :1536]), z_sw], axis=1)
        if nakv_ref is not None:
            _write_state(nakv_ref, j, slot, z[:, 512:1536])
            _write_state(swkv_ref, j, slot, z[:, 2048:2304])
        scale = HEAD_DIM ** -0.5 * LOG2E
        qrows, krows = pl.ds(j * n + r0, TM), pl.ds(j * nk + r0, TM)
        for p in range(NA_HEADS // 2):
            qp = z[:, LANES * p:LANES * (p + 1)] * scale
            qna[2 * p, qrows, :] = jnp.where(lo, qp, 0.0).astype(BF16)
            qna[2 * p + 1, qrows, :] = jnp.where(lo, 0.0, qp).astype(BF16)
            kna[p, krows, :] = z[:, 512 + LANES * p:512 + LANES * (p + 1)].astype(BF16)
            vna[p, krows, :] = z[:, 1024 + LANES * p:1024 + LANES * (p + 1)].astype(BF16)
        if latent:
            cos, sp, sn = cos_ref[...], sp_ref[...], sn_ref[...]
        for p in range(SWA_HEADS // 2):
            qp = z[:, 1536 + LANES * p:1536 + LANES * (p + 1)]
            if latent:
                qp = _rope(qp, cos, sp, sn, 16)
            qp = qp * scale
            qsw[2 * p, qrows, :] = jnp.where(lo, qp, 0.0).astype(BF16)
            qsw[2 * p + 1, qrows, :] = jnp.where(lo, 0.0, qp).astype(BF16)
        k = z[:, 2048:2176]
        if latent:
            k = _rope(k, cos, sp, sn, 16)
        v = z[:, 2176:2304]
        ksw_, vsw_ = pltpu.roll(k, HEAD_DIM, 1), pltpu.roll(v, HEAD_DIM, 1)
        swrows = pl.ds(j * nsw + sw_off + r0, TM)
        ksw[0, swrows, :] = jnp.where(lo, k, ksw_).astype(BF16)
        ksw[1, swrows, :] = jnp.where(lo, ksw_, k).astype(BF16)
        vsw[0, swrows, :] = jnp.where(lo, v, vsw_).astype(BF16)
        vsw[1, swrows, :] = jnp.where(lo, vsw_, v).astype(BF16)

    def first():
        cna = cna_ref[0, 0]
        for p in range(NA_HEADS // 2):
            kna[p, n:n + PAST_LEN, :] = cna[:, LANES * p:LANES * (p + 1)].astype(BF16)
            vna[p, n:n + PAST_LEN, :] = cna[:, 512 + LANES * p:512 + LANES * (p + 1)].astype(BF16)
        csw = csw_ref[0, 0]
        lo_c = lax.broadcasted_iota(jnp.int32, (PAST_LEN, LANES), 1) < HEAD_DIM
        zpad = jnp.zeros((SWA_WINDOW, LANES), BF16)
        c0 = n + 2 * SWA_WINDOW
        for ref, val in ((ksw, csw[:, :LANES]), (vsw, csw[:, LANES:])):
            swapped = pltpu.roll(val, HEAD_DIM, 1)
            ref[0, c0:c0 + PAST_LEN, :] = jnp.where(lo_c, val, swapped).astype(BF16)
            ref[1, c0:c0 + PAST_LEN, :] = jnp.where(lo_c, swapped, val).astype(BF16)
            for kv in range(SWA_KV_HEADS):
                ref[kv, 0:SWA_WINDOW, :] = zpad
                ref[kv, n + SWA_WINDOW:c0, :] = zpad

    def attend(j, ti, xq, write):
        q0 = ti * TM if isinstance(ti, int) else pl.multiple_of(ti * TM, TM)
        rows_per_tile = TM // GRID_W
        nrows = n // GRID_W
        nloc = NA_KH * GRID_W

        def na_pair(p):
            if latent:
                blk = 2 * GRID_W
                qst = jnp.concatenate(
                    [qna[2 * p + a, pl.ds(j * n + q0 + GRID_W * r, GRID_W), :]
                     for r in range(rows_per_tile) for a in (0, 1)], axis=0)
                s_loc, starts = [], []
                for r in range(rows_per_tile):
                    row = ti * rows_per_tile + r
                    rs = jnp.clip(row - NA_KH // 2, 0, nrows - NA_KH)
                    k0 = pl.multiple_of(rs * GRID_W, GRID_W)
                    starts.append(k0)
                    var = rs - row + NA_KH - 1
                    bias = jnp.concatenate([bias_ref[0, 2 * p, var], bias_ref[0, 2 * p + 1, var]], axis=0)
                    s_loc.append(_dot_t(qst[blk * r:blk * (r + 1)], kna[p, pl.ds(k0, nloc), :]) + bias)
                s_loc = jnp.concatenate(s_loc, axis=0)
                s_ctx = _dot_t(qst, kna[p, pl.ds(n, PAST_LEN), :])

                def pv_loc(e):
                    return jnp.concatenate(
                        [_dot(e[blk * r:blk * (r + 1)], vna[p, pl.ds(starts[r], nloc), :])
                         for r in range(rows_per_tile)], axis=0)

                o = _softmax_pv([(s_loc, pv_loc), (s_ctx, vna[p, pl.ds(n, PAST_LEN), :])])
                outs = [jnp.concatenate([o[blk * r + GRID_W * a:blk * r + GRID_W * (a + 1)]
                                         for r in range(rows_per_tile)], axis=0) for a in (0, 1)]
            else:
                qst = jnp.concatenate([qna[2 * p + a, pl.ds(j * n + q0, TM), :] for a in (0, 1)], axis=0)
                o = _softmax_pv([(_dot_t(qst, kna[p, j * nk:(j + 1) * nk, :]), vna[p, j * nk:(j + 1) * nk, :])])
                outs = [o[:TM], o[TM:]]
            osc[j, p] = jnp.where(lo, outs[0], outs[1]).astype(BF16)

        group = SWA_HEADS // SWA_KV_HEADS
        stack = SWA_STACK

        if latent:
            span = TM + 2 * SWA_WINDOW
            a_i = lax.broadcasted_iota(jnp.int32, (TM, span), 0)
            b_i = lax.broadcasted_iota(jnp.int32, (TM, span), 1)
            key = q0 - SWA_WINDOW + b_i
            ok = (b_i >= a_i) & (b_i <= a_i + 2 * SWA_WINDOW) & (key >= 0) & (key < n)
            band = jnp.where(ok, 0.0, NEG)

        def swa_chunk(c):
            kv = c * stack // group
            qst = jnp.concatenate([qsw[stack * c + g, pl.ds(j * n + q0, TM), :] for g in range(stack)], axis=0)
            sink = jnp.concatenate(
                [jnp.full((TM, 1), sink_ref[slot, stack * c + g] * LOG2E, F32) for g in range(stack)], axis=0)
            if latent:
                s_loc = _dot_t(qst, ksw[kv, pl.ds(q0, span), :])
                s_loc = jnp.concatenate(
                    [s_loc[TM * g:TM * (g + 1)] + band for g in range(stack)], axis=0)
                c0 = n + 2 * SWA_WINDOW
                s_ctx = _dot_t(qst, ksw[kv, pl.ds(c0, PAST_LEN), :])
                o = _softmax_pv([(s_loc, vsw[kv, pl.ds(q0, span), :]),
                                 (s_ctx, vsw[kv, pl.ds(c0, PAST_LEN), :])], extra=sink)
            else:
                o = _softmax_pv([(_dot_t(qst, ksw[kv, j * nsw:(j + 1) * nsw, :]),
                                  vsw[kv, j * nsw:(j + 1) * nsw, :])], extra=sink)
            for u in range(stack // 2):
                osc[j, NA_HEADS // 2 + (stack // 2) * c + u] = jnp.where(
                    lo, o[TM * 2 * u:TM * (2 * u + 1)], o[TM * (2 * u + 1):TM * (2 * u + 2)]).astype(BF16)

        if latent:
            def na_step(i, carry):
                na_pair(2 * i)
                na_pair(2 * i + 1)
                return carry
            lax.fori_loop(0, NA_HEADS // 4, na_step, 0)
        else:
            for p in range(NA_HEADS // 2):
                na_pair(p)
        for c in range(SWA_HEADS // stack):
            swa_chunk(c)

        mix = jnp.concatenate([osc[j, p] for p in range(8)], axis=1)
        write(xq + m[2:3] * _dot(mix, wout_ref[0]))

    if latent:
        s = pl.program_id(1)

        @pl.when(s < nt)
        def _():
            project(0, pl.multiple_of(s * TM, TM), x_ref[...])

            @pl.when(s == 0)
            def _():
                first()

        @pl.when(s >= nt)
        def _():
            def write(val):
                o_ref[...] = val
            attend(0, s - nt, x_ref[...], write)
    else:
        for j in range(sps):
            project(j, 0, x_ref[j * n:(j + 1) * n, :])

        @pl.when(pl.program_id(0) >= 0)
        def _():
            for j in range(sps):
                def write(val, j=j):
                    o_ref[j * n:(j + 1) * n, :] = val
                attend(j, 0, x_ref[j * n:(j + 1) * n, :], write)


def _odd_mixer(x, mod, l, n, latent, g_mix, w, tables, bias, cna, csw, n_slots=1, prev=None):
    slot = l // 2
    t = x.shape[0]
    nb = t // n
    nt = n // TM
    nk = n + (PAST_LEN if latent else 0)
    nsw = n + (2 * SWA_WINDOW + PAST_LEN if latent else 0)
    sps = 1 if latent else CTX_SEQS_PER_STEP
    weights = [_layer_spec(g_mix, l), _layer_spec(w["w_in"], slot), pl.BlockSpec(memory_space=pltpu.SMEM),
               _layer_spec(w["w_out"], slot)]
    args = [x, mod, g_mix, w["w_in"], w["sink"], w["w_out"]]
    out_shape = [jax.ShapeDtypeStruct((t, D_MODEL), F32)]
    aliases = {}
    if latent:
        grid = (nb, 2 * nt)
        tspec = pl.BlockSpec((TM, LANES), lambda b, s: (jnp.minimum(s, nt - 1), 0))
        in_specs = [pl.BlockSpec((TM, D_MODEL), lambda b, s: (b * nt + s % nt, 0)),
                    pl.BlockSpec((1, 1, N_MOD, D_MODEL), lambda b, s: (l, 1 + b, 0, 0)),
                    *weights, tspec, tspec, tspec, _layer_spec(bias, slot),
                    pl.BlockSpec((1, 1, PAST_LEN, 1024), lambda b, s: (b, slot, 0, 0)),
                    pl.BlockSpec((1, 1, PAST_LEN, 2 * SWA_KV_HEADS * HEAD_DIM), lambda b, s: (b, slot, 0, 0))]
        args += [tables["cos"], tables["sin_prev"], tables["sin_next"], bias, cna, csw]
        out_specs = [pl.BlockSpec((TM, D_MODEL), lambda b, s: (b * nt + jnp.maximum(s - nt, 0), 0))]
    else:
        grid = (nb // sps,)
        in_specs = [pl.BlockSpec((sps * n, D_MODEL), lambda i: (i, 0)),
                    pl.BlockSpec((1, 1, N_MOD, D_MODEL), lambda i: (l, 0, 0, 0)), *weights]
        out_specs = [pl.BlockSpec((sps * n, D_MODEL), lambda i: (i, 0))]
        for k, width in enumerate((2 * NA_HEADS * HEAD_DIM, 2 * SWA_KV_HEADS * HEAD_DIM)):
            shape, spec = _state_specs(nb, n_slots, n, width, slot, prev, sps)
            out_shape.append(shape)
            out_specs.append(spec)
            if prev is not None:
                aliases[len(args)] = 1 + k
                in_specs.append(pl.BlockSpec(memory_space=pl.ANY))
                args.append(prev[k])
    scratch = [
        pltpu.VMEM((NA_HEADS, sps * n, LANES), BF16),
        pltpu.VMEM((NA_HEADS // 2, sps * nk, LANES), BF16),
        pltpu.VMEM((NA_HEADS // 2, sps * nk, LANES), BF16),
        pltpu.VMEM((SWA_HEADS, sps * n, LANES), BF16),
        pltpu.VMEM((SWA_KV_HEADS, sps * nsw, LANES), BF16),
        pltpu.VMEM((SWA_KV_HEADS, sps * nsw, LANES), BF16),
        pltpu.VMEM((sps, 8, TM, LANES), BF16),
    ]
    return pl.pallas_call(
        functools.partial(_odd_body, n=n, latent=latent, slot=slot, sps=sps),
        out_shape=out_shape,
        grid=grid,
        in_specs=in_specs,
        out_specs=out_specs,
        scratch_shapes=scratch,
        input_output_aliases=aliases,
        compiler_params=_cparams(("arbitrary",) * len(grid)),
        name="odd_latent" if latent else "odd_context",
    )(*args)


def _rope_tables(n, rot, lane0, period):
    half = rot // 2
    nf = half // 2
    t = np.arange(n)
    freqs = (ROPE_BASE ** (-np.arange(nf, dtype=np.float32) / nf)).astype(np.float32)
    cos = np.ones((n, period), np.float32)
    sp = np.zeros((n, period), np.float32)
    sn = np.zeros((n, period), np.float32)
    for k, pos in enumerate((t // GRID_W, t % GRID_W)):
        ang = pos.astype(np.float32)[:, None] * freqs
        c, s_ = np.cos(ang), np.sin(ang)
        a = lane0 + k * half
        cos[:, a:a + nf] = c
        cos[:, a + nf:a + half] = c
        sn[:, a:a + nf] = -s_
        sp[:, a + nf:a + half] = s_
    reps = LANES // period
    return {"cos": jnp.asarray(np.tile(cos, (1, reps))), "sin_prev": jnp.asarray(np.tile(sp, (1, reps))),
            "sin_next": jnp.asarray(np.tile(sn, (1, reps)))}


def _pool_counts(n):
    t = np.arange(n)
    cols = []
    for wdw in POOL_WINDOWS:
        lo = np.clip(t - wdw // 2, 0, n)
        hi = np.clip(t + wdw // 2, 0, n)
        cols.append(np.repeat((hi - lo).astype(np.float32)[:, None], POOL_GROUP, axis=1))
    return jnp.asarray(np.concatenate(cols, axis=1))


def _even_weights(w_in, g_q, g_kv, w_uq, w_ukv, w_pool, pool_scale, w_out):
    ne = w_in.shape[0]
    b_, c_ = Q_LORA + KV_LORA, Q_LORA + KV_LORA + QK_ROPE
    krpad = jnp.pad(w_in[..., b_:c_], ((0, 0), (0, 0), (64, LANES - 64 - QK_ROPE)))
    w_in_p = jnp.concatenate([w_in[..., :b_], w_in[..., c_:], krpad], axis=-1)
    uq = w_uq.reshape(ne, Q_LORA, MLA_HEADS, QK_NOPE + QK_ROPE)
    uq = jnp.pad(uq, ((0, 0),) * 3 + ((0, LANES - QK_NOPE - QK_ROPE),)).reshape(ne, Q_LORA, MLA_HEADS * LANES)
    ukv = w_ukv.reshape(ne, KV_LORA, MLA_HEADS, QK_NOPE + V_DIM)
    uk = jnp.pad(ukv[..., :QK_NOPE], ((0, 0),) * 3 + ((0, LANES - QK_NOPE),))
    sel = np.zeros((LANES, MLA_HEADS, LANES), np.float32)
    sel[64 + np.arange(QK_ROPE), :, 64 + np.arange(QK_ROPE)] = 1.0
    sel = jnp.broadcast_to(jnp.asarray(sel), (ne,) + sel.shape)
    w_kcat = jnp.concatenate([uk, sel], axis=1).reshape(ne, KV_LORA + LANES, MLA_HEADS * LANES)
    w_uv = ukv[..., QK_NOPE:].reshape(ne, KV_LORA, MLA_HEADS * V_DIM)
    eye = jnp.asarray(np.eye(len(POOL_WINDOWS), dtype=np.float32))
    w_pool_bd = (w_pool[:, :, :, None, :] * eye[None, :, None, :, None]).reshape(ne, POOL_DIM, POOL_DIM)
    return {"w_in": w_in_p.astype(BF16), "g_q": g_q.reshape(ne, 1, -1), "g_kv": g_kv.reshape(ne, 1, -1),
            "w_uq": uq.astype(BF16), "w_kcat": w_kcat.astype(BF16), "w_uv": w_uv.astype(BF16),
            "w_pool": w_pool_bd.astype(BF16), "pool_scale": pool_scale.reshape(ne, 1, -1),
            "w_out": w_out.astype(BF16)}


def kernel(x_prompt, x_sample, cache_mla_latent, cache_na_kv, cache_swa_kv, c, c_ctx, w_mod, b_mod, norm_mix, norm_ffn, norm_final, w_in_even, mla_q_norm, mla_kv_norm, w_uq, w_ukv, w_pool, pool_scale, w_out_even, w_in_odd, na_rpb, swa_sink, w_out_odd, w_up, conv_w, conv_b, w_down):
    nbp, n_p, _ = x_prompt.shape
    nbs, n_s, _ = x_sample.shape
    depth = w_mod.shape[0]
    xp = x_prompt.reshape(nbp * n_p, D_MODEL)
    xs = x_sample.reshape(nbs * n_s, D_MODEL)

    c_all = jnp.zeros((8, D_MODEL), F32).at[0].set(c_ctx).at[1:1 + nbs].set(c)
    mod = _adaln(c_all, w_mod, b_mod).reshape(depth, 8, N_MOD, D_MODEL)

    tab_mla = _rope_tables(n_s, QK_ROPE, 64, LANES)
    tab_swa = _rope_tables(n_s, HEAD_DIM, 0, HEAD_DIM)
    cnt_p, cnt_s = _pool_counts(n_p), _pool_counts(n_s)

    bias_all = _na_bias(na_rpb.reshape((-1,) + na_rpb.shape[2:])).reshape(
        na_rpb.shape[:2] + (NA_KH, GRID_W, NA_KH * GRID_W))

    n_even, n_odd = (depth + 1) // 2, depth // 2
    g_mix = norm_mix.reshape(depth, 1, D_MODEL)
    w_even = _even_weights(w_in_even, mla_q_norm, mla_kv_norm, w_uq, w_ukv, w_pool, pool_scale, w_out_even)
    w_odd = {"w_in": w_in_odd.astype(BF16), "w_out": w_out_odd.astype(BF16), "sink": swa_sink}
    w_ffn = {"g_ffn": norm_ffn.reshape(depth, 1, D_MODEL), "w_up": w_up.astype(BF16), "conv_w": conv_w,
             "conv_b": conv_b.reshape(depth, 1, -1), "w_down": w_down.astype(BF16),
             "g_final": norm_final.reshape(1, -1)}
    ctx = cache_mla_latent
    cna = cache_na_kv.reshape(nbs, -1, PAST_LEN, 2 * NA_HEADS * HEAD_DIM)
    csw = cache_swa_kv.reshape(nbs, -1, PAST_LEN, 2 * SWA_KV_HEADS * HEAD_DIM)

    lat = kv = None
    for l in range(depth):
        if l % 2 == 0:
            xp, lat = _even_mixer(xp, mod, l, n_p, False, g_mix, w_even, {"cnt": cnt_p}, None,
                                  n_slots=n_even, prev_lat=lat)
            (xs,) = _even_mixer(xs, mod, l, n_s, True, g_mix, w_even, dict(tab_mla, cnt=cnt_s), ctx)
        else:
            xp, *kv = _odd_mixer(xp, mod, l, n_p, False, g_mix, w_odd, None, None, None, None,
                                 n_slots=n_odd, prev=kv)
            (xs,) = _odd_mixer(xs, mod, l, n_s, True, g_mix, w_odd, tab_swa, bias_all, cna, csw)
        xp = _ffn(xp, n_p, False, mod, l, w_ffn, l == depth - 1)
        xs = _ffn(xs, n_s, True, mod, l, w_ffn, l == depth - 1)

    return (xp.reshape(nbp, n_p, D_MODEL), xs.reshape(nbs, n_s, D_MODEL), lat,
            kv[0].reshape(nbp, n_odd, n_p, 2, NA_HEADS, HEAD_DIM),
            kv[1].reshape(nbp, n_odd, n_p, 2, SWA_KV_HEADS, HEAD_DIM))
```
